```python
import jax, jax.numpy as jnp
from jax import lax
import numpy as np

D_MODEL = 1024
BATCH = 2
SEQ = 8192
DEPTH = 4
DEC_BATCH = 32
DEC_SEQ = 2048
PAST_LEN = 128

N_MIXERS = 2
N_MLA = (DEPTH + 1) // 2
N_HGRN = DEPTH // 2
PLE_DIM = 256
MLA_HEADS = 8
MLA_NOPE = 128
MLA_ROPE = 64
MLA_V = 128
MLA_Q_LORA = 384
MLA_KV_LORA = 256
MLA_SCALE = (MLA_NOPE + MLA_ROPE) ** -0.5
ROPE_THETA = 10000.0
Q_BLOCK = 128
HGRN_HEADS = 8
HGRN_DK = 128
HGRN_DV = 128
HGRN_CHUNK = 32
GATE_FLOOR = 1e-30
PEER_HEADS = 8
PEER_DK = 256
N_KEYS = 128
N_EXPERTS = N_KEYS * N_KEYS
PEER_TOPK = 16
PEER_BLOCK = 128
DEEPNORM_ALPHA = (2 * DEPTH) ** 0.25
DEEPNORM_BETA = (8 * DEPTH) ** -0.25
LN_EPS = 1e-5
RMS_EPS = 1e-6

kernel_name = 'hybrid_mla_hgrn2_peer_encoder'


def layer_norm(x, g, b):
    xf = x.astype(jnp.float32)
    mu = jnp.mean(xf, axis=-1, keepdims=True)
    var = jnp.mean(jnp.square(xf - mu), axis=-1, keepdims=True)
    y = (xf - mu) * lax.rsqrt(var + LN_EPS) * g.astype(jnp.float32) + b.astype(jnp.float32)
    return y.astype(x.dtype)


def rms_norm(x, w):
    xf = x.astype(jnp.float32)
    return xf * lax.rsqrt(jnp.mean(jnp.square(xf), axis=-1, keepdims=True) + RMS_EPS) * w.astype(jnp.float32)


def rope_tables(seq_len):
    inv_freq = 1.0 / (ROPE_THETA ** (jnp.arange(0, MLA_ROPE, 2, dtype=jnp.float32) / MLA_ROPE))
    ang = jnp.arange(seq_len, dtype=jnp.float32)[:, None] * inv_freq[None, :]
    emb = jnp.concatenate([ang, ang], axis=-1)
    return jnp.cos(emb), jnp.sin(emb)


def rotate_half(t):
    t1, t2 = jnp.split(t, 2, axis=-1)
    return jnp.concatenate([-t2, t1], axis=-1)


def apply_rope(t, cos, sin):
    return t * cos.astype(t.dtype) + rotate_half(t) * sin.astype(t.dtype)


def bidir_attention(q, k, v):
    B, S, H, Dq = q.shape
    nb = S // Q_BLOCK
    qb = q.reshape(B, nb, Q_BLOCK, H, Dq).transpose(1, 0, 2, 3, 4)

    def one_block(qc):
        s = jnp.einsum('bqhd,bkhd->bhqk', qc, k, preferred_element_type=jnp.float32) * MLA_SCALE
        pr = jax.nn.softmax(s, axis=-1).astype(v.dtype)
        return jnp.einsum('bhqk,bkhd->bqhd', pr, v)

    o = lax.map(one_block, qb)
    return o.transpose(1, 0, 2, 3, 4).reshape(B, S, H, v.shape[-1])


def mla_mixer(x, w_a, q_norm, kv_norm, w_uq, w_ukv, w_o, cos, sin):
    B, S, _ = x.shape
    a = x @ w_a
    c_q = a[..., :MLA_Q_LORA]
    c_kv = a[..., MLA_Q_LORA:MLA_Q_LORA + MLA_KV_LORA]
    k_rope = a[..., MLA_Q_LORA + MLA_KV_LORA:]
    q = (rms_norm(c_q, q_norm).astype(x.dtype) @ w_uq).reshape(B, S, MLA_HEADS, MLA_NOPE + MLA_ROPE)
    kv = (rms_norm(c_kv, kv_norm).astype(x.dtype) @ w_ukv).reshape(B, S, MLA_HEADS, MLA_NOPE + MLA_V)
    q_nope, q_rope = q[..., :MLA_NOPE], q[..., MLA_NOPE:]
    k_nope, v = kv[..., :MLA_NOPE], kv[..., MLA_NOPE:]
    q_rope = apply_rope(q_rope, cos[None, :, None, :], sin[None, :, None, :])
    k_rope = apply_rope(k_rope, cos[None], sin[None])
    q = jnp.concatenate([q_nope, q_rope], axis=-1)
    k = jnp.concatenate([k_nope, jnp.broadcast_to(k_rope[:, :, None, :], (B, S, MLA_HEADS, MLA_ROPE))], axis=-1)
    o = bidir_attention(q, k, v)
    return o.reshape(B, S, MLA_HEADS * MLA_V) @ w_o


def gla_chunk_scan(q, k, v, log_f):
    B, S, H, dk = q.shape
    dv = v.shape[-1]
    nc = S // HGRN_CHUNK

    def to_chunks(t):
        return t.astype(jnp.float32).reshape(B, nc, HGRN_CHUNK, H, t.shape[-1]).transpose(1, 0, 3, 2, 4)

    lower = jnp.tril(jnp.ones((HGRN_CHUNK, HGRN_CHUNK), dtype=bool))[:, :, None]

    def step(state, inp):
        qc, kc, vc, gc = inp
        b = jnp.cumsum(gc, axis=-2)
        diff = b[..., :, None, :] - b[..., None, :, :]
        decay = jnp.where(lower, jnp.exp(jnp.minimum(diff, 0.0)), 0.0)
        attn = jnp.einsum('bhtk,bhtsk->bhts', qc, decay * kc[..., None, :, :])
        o = jnp.einsum('bhts,bhsv->bhtv', attn, vc) + jnp.einsum('bhtk,bhkv->bhtv', qc * jnp.exp(b), state)
        b_last = b[..., -1:, :]
        state = jnp.exp(b_last[..., 0, :])[..., None] * state + jnp.einsum('bhsk,bhsv->bhkv', kc * jnp.exp(b_last - b), vc)
        return state, o

    state0 = jnp.zeros((B, H, dk, dv), jnp.float32)
    _, o = lax.scan(step, state0, (to_chunks(q), to_chunks(k), to_chunks(v), to_chunks(log_f)))
    return o.transpose(1, 0, 3, 2, 4).reshape(B, S, H, dv)


def hgrn2_mixer(x, w_in, lb_f, lb_b, norm_w, w_o):
    B, S, _ = x.shape
    hk = HGRN_HEADS * HGRN_DK
    hv = HGRN_HEADS * HGRN_DV
    proj = x @ w_in
    q = jax.nn.silu(proj[..., :hk]).reshape(B, S, HGRN_HEADS, HGRN_DK)
    z_f = proj[..., hk:2 * hk]
    z_b = proj[..., 2 * hk:3 * hk]
    v = proj[..., 3 * hk:3 * hk + hv].reshape(B, S, HGRN_HEADS, HGRN_DV)
    g = proj[..., 3 * hk + hv:].reshape(B, S, HGRN_HEADS, HGRN_DV)

    def gates(z, lb):
        z = z.astype(jnp.float32)
        f = lb + (1.0 - lb) * jax.nn.sigmoid(z)
        log_f = jnp.log(jnp.maximum(f, GATE_FLOOR))
        k = (1.0 - lb) * jax.nn.sigmoid(-z)
        return log_f.reshape(B, S, HGRN_HEADS, HGRN_DK), k.reshape(B, S, HGRN_HEADS, HGRN_DK)

    g_f, k_f = gates(z_f, lb_f)
    g_b, k_b = gates(z_b, lb_b)
    flip = lambda t: jnp.flip(t, axis=1)
    o = gla_chunk_scan(q, k_f, v, g_f) + flip(gla_chunk_scan(flip(q), flip(k_b), flip(v), flip(g_b)))
    o = rms_norm(o, norm_w) * jax.nn.silu(g.astype(jnp.float32))
    return o.reshape(B, S, hv).astype(x.dtype) @ w_o


def peer_ffn(x, w_q, sub_keys, u, v):
    B, S, D = x.shape
    xb = x.reshape(B * S // PEER_BLOCK, PEER_BLOCK, D)
    kk = PEER_TOPK * PEER_TOPK

    def one_block(xc):
        q = (xc @ w_q).reshape(PEER_BLOCK, PEER_HEADS, 2, PEER_DK // 2)
        s = jnp.einsum('thcd,cnd->thcn', q, sub_keys, preferred_element_type=jnp.float32)
        sv, si = lax.top_k(s, PEER_TOPK)
        cand = (sv[:, :, 0, :, None] + sv[:, :, 1, None, :]).reshape(PEER_BLOCK, PEER_HEADS, kk)
        cidx = (si[:, :, 0, :, None] * N_KEYS + si[:, :, 1, None, :]).reshape(PEER_BLOCK, PEER_HEADS, kk)
        top_s, pos = lax.top_k(cand, PEER_TOPK)
        idx = jnp.take_along_axis(cidx, pos, axis=-1)
        gate = jax.nn.softmax(top_s, axis=-1)
        ue = jnp.take(u, idx, axis=0)
        ve = jnp.take(v, idx, axis=0)
        h = jax.nn.gelu(jnp.einsum('thkd,td->thk', ue, xc, preferred_element_type=jnp.float32), approximate=False)
        return jnp.einsum('thk,thkd->td', (gate * h).astype(x.dtype), ve)

    return lax.map(one_block, xb).reshape(B, S, D)


def trunk(x, p, lower_bounds, mla_w_a, mla_q_norm, mla_kv_norm, mla_w_uq, mla_w_ukv, mla_w_o,
          hgrn_w_in, hgrn_norm, hgrn_w_o, peer_w_q, peer_sub_keys, peer_u, peer_v,
          ln1_g, ln1_b, ln2_g, ln2_b, ple_gate_w, ple_gate_b, ple_proj):
    cos, sin = rope_tables(x.shape[1])
    for i in range(DEPTH):
        j = i // N_MIXERS
        if i % N_MIXERS == 0:
            h = mla_mixer(x, mla_w_a[j], mla_q_norm[j], mla_kv_norm[j], mla_w_uq[j], mla_w_ukv[j], mla_w_o[j], cos, sin)
        else:
            h = hgrn2_mixer(x, hgrn_w_in[j], lower_bounds[0, j], lower_bounds[1, j], hgrn_norm[j], hgrn_w_o[j])
        x = layer_norm(DEEPNORM_ALPHA * x + h, ln1_g[i], ln1_b[i])
        x = layer_norm(DEEPNORM_ALPHA * x + peer_ffn(x, peer_w_q[i], peer_sub_keys[i], peer_u[i], peer_v[i]), ln2_g[i], ln2_b[i])
        gate = jax.nn.sigmoid(x @ ple_gate_w[i] + ple_gate_b[i])
        x = x + gate * (p[i] @ ple_proj[i])
    return x


def setup_inputs(seed: int = 0) -> dict:
    key = jax.random.key(seed)
    ks = jax.random.split(key, 32)
    f32 = jnp.float32

    def nrm(k, shape, scale):
        return jax.random.normal(k, shape, f32) * scale

    hk = HGRN_HEADS * HGRN_DK
    hv = HGRN_HEADS * HGRN_DV
    ukv_scale = jnp.concatenate([jnp.ones((MLA_NOPE,), f32), jnp.full((MLA_V,), DEEPNORM_BETA, f32)])
    hin_scale = jnp.concatenate([jnp.ones((3 * hk,), f32), jnp.full((hv,), DEEPNORM_BETA, f32), jnp.ones((hv,), f32)])
    return {
        'x_prompt': nrm(ks[0], (BATCH, SEQ, D_MODEL), 1.0),
        'x_sample': nrm(ks[1], (DEC_BATCH, DEC_SEQ, D_MODEL), 1.0),
        'p_prompt': nrm(ks[2], (DEPTH, BATCH, SEQ, PLE_DIM), 1.0),
        'p_sample': nrm(ks[3], (DEPTH, DEC_BATCH, DEC_SEQ, PLE_DIM), 1.0),
        'mla_w_a': nrm(ks[4], (N_MLA, D_MODEL, MLA_Q_LORA + MLA_KV_LORA + MLA_ROPE), D_MODEL ** -0.5),
        'mla_q_norm': 1.0 + nrm(ks[5], (N_MLA, MLA_Q_LORA), 0.02),
        'mla_kv_norm': 1.0 + nrm(ks[6], (N_MLA, MLA_KV_LORA), 0.02),
        'mla_w_uq': nrm(ks[7], (N_MLA, MLA_Q_LORA, MLA_HEADS * (MLA_NOPE + MLA_ROPE)), MLA_Q_LORA ** -0.5),
        'mla_w_ukv': (nrm(ks[8], (N_MLA, MLA_KV_LORA, MLA_HEADS, MLA_NOPE + MLA_V), MLA_KV_LORA ** -0.5) * ukv_scale).reshape(N_MLA, MLA_KV_LORA, MLA_HEADS * (MLA_NOPE + MLA_V)),
        'mla_w_o': nrm(ks[9], (N_MLA, MLA_HEADS * MLA_V, D_MODEL), DEEPNORM_BETA * (MLA_HEADS * MLA_V) ** -0.5),
        'hgrn_w_in': nrm(ks[10], (N_HGRN, D_MODEL, 3 * hk + 2 * hv), D_MODEL ** -0.5) * hin_scale,
        'hgrn_lb': nrm(ks[11], (2, N_HGRN, hk), 0.5),
        'hgrn_norm': 1.0 + nrm(ks[12], (N_HGRN, HGRN_DV), 0.02),
        'hgrn_w_o': nrm(ks[13], (N_HGRN, hv, D_MODEL), DEEPNORM_BETA * hv ** -0.5),
        'peer_w_q': nrm(ks[14], (DEPTH, D_MODEL, PEER_HEADS * PEER_DK), D_MODEL ** -0.5),
        'peer_sub_keys': nrm(ks[15], (DEPTH, 2, N_KEYS, PEER_DK // 2), (PEER_DK // 2) ** -0.5),
        'peer_u': nrm(ks[16], (DEPTH, N_EXPERTS, D_MODEL), D_MODEL ** -0.5),
        'peer_v': nrm(ks[17], (DEPTH, N_EXPERTS, D_MODEL), DEEPNORM_BETA * PEER_HEADS ** -0.5),
        'ln1_g': 1.0 + nrm(ks[18], (DEPTH, D_MODEL), 0.02),
        'ln1_b': nrm(ks[19], (DEPTH, D_MODEL), 0.02),
        'ln2_g': 1.0 + nrm(ks[20], (DEPTH, D_MODEL), 0.02),
        'ln2_b': nrm(ks[21], (DEPTH, D_MODEL), 0.02),
        'ple_gate_w': nrm(ks[22], (DEPTH, D_MODEL, D_MODEL), D_MODEL ** -0.5),
        'ple_gate_b': nrm(ks[23], (DEPTH, D_MODEL), 0.02),
        'ple_proj': nrm(ks[24], (DEPTH, PLE_DIM, D_MODEL), PLE_DIM ** -0.5),
    }


def reference(x_prompt, x_sample, p_prompt, p_sample, mla_w_a, mla_q_norm, mla_kv_norm, mla_w_uq, mla_w_ukv, mla_w_o,
              hgrn_w_in, hgrn_lb, hgrn_norm, hgrn_w_o, peer_w_q, peer_sub_keys, peer_u, peer_v,
              ln1_g, ln1_b, ln2_g, ln2_b, ple_gate_w, ple_gate_b, ple_proj):
    sm = jax.nn.softmax(hgrn_lb.astype(jnp.float32), axis=1)
    lower_bounds = jnp.cumsum(sm, axis=1) - sm[:, :1]
    y_prompt = trunk(x_prompt, p_prompt, lower_bounds, mla_w_a, mla_q_norm, mla_kv_norm, mla_w_uq, mla_w_ukv, mla_w_o,
                     hgrn_w_in, hgrn_norm, hgrn_w_o, peer_w_q, peer_sub_keys, peer_u, peer_v,
                     ln1_g, ln1_b, ln2_g, ln2_b, ple_gate_w, ple_gate_b, ple_proj)
    y_sample = trunk(x_sample, p_sample, lower_bounds, mla_w_a, mla_q_norm, mla_kv_norm, mla_w_uq, mla_w_ukv, mla_w_o,
                     hgrn_w_in, hgrn_norm, hgrn_w_o, peer_w_q, peer_sub_keys, peer_u, peer_v,
                     ln1_g, ln1_b, ln2_g, ln2_b, ple_gate_w, ple_gate_b, ple_proj)
    return (y_prompt, y_sample)
```

```python
import functools
import math

import numpy as np
import jax
import jax.numpy as jnp
from jax import lax
from jax.experimental import pallas as pl
from jax.experimental.pallas import tpu as pltpu

D_MODEL = 1024
PLE_DIM = 256
MLA_HEADS = 8
MLA_NOPE = 128
MLA_ROPE = 64
MLA_V = 128
MLA_Q_LORA = 384
MLA_KV_LORA = 256
MLA_SCALE = (MLA_NOPE + MLA_ROPE) ** -0.5
MLA_QK_PAD = 256
ROPE_THETA = 10000.0
HGRN_HEADS = 8
HGRN_DK = 128
HGRN_DV = 128
GATE_FLOOR = 1e-30
PEER_HEADS = 8
PEER_DK = 256
N_KEYS = 128
N_EXPERTS = N_KEYS * N_KEYS
PEER_TOPK = 16
LN_EPS = 1e-5
RMS_EPS = 1e-6

LANES = 128
VMEM_LIMIT = 56 * 1024 * 1024

SCAN_CHUNK = 128
SCAN_LEVELS = 7
W_PITCH = 136

F32 = jnp.float32
BF16 = jnp.bfloat16

_NT = (((1,), (1,)), ((), ()))


def _params(*sem):
    return pltpu.CompilerParams(dimension_semantics=sem, vmem_limit_bytes=VMEM_LIMIT)


def _dot(a, b):
    return jnp.dot(a, b, preferred_element_type=F32)


def _dot_nt(a, b):
    return lax.dot_general(a, b, _NT, preferred_element_type=F32)


def _layer_norm(y, g, b):
    mu = jnp.mean(y, axis=-1, keepdims=True)
    yc = y - mu
    var = jnp.mean(yc * yc, axis=-1, keepdims=True)
    return yc * lax.rsqrt(var + LN_EPS) * g + b


def _silu(t):
    return t * (1.0 / (1.0 + jnp.exp(-t)))


def _full(shape):
    nd = len(shape)
    return pl.BlockSpec(shape, lambda *_: (0,) * nd)


def _lower_bounds_kernel(lb_ref, out_ref):
    n = lb_ref.shape[1]
    for d in range(lb_ref.shape[0]):
        rows = [lb_ref[d, j] for j in range(n)]
        m = rows[0]
        for r in rows[1:]:
            m = jnp.maximum(m, r)
        ex = [jnp.exp(r - m) for r in rows]
        tot = ex[0]
        for e in ex[1:]:
            tot = tot + e
        sm = [e / tot for e in ex]
        cum = sm[0]
        out_ref[d, 0] = cum - sm[0]
        for j in range(1, n):
            cum = cum + sm[j]
            out_ref[d, j] = cum - sm[0]


def lower_bounds(hgrn_lb):
    two, n, hk = hgrn_lb.shape
    lb4 = hgrn_lb.astype(F32).reshape(two, n, 1, hk)
    out = pl.pallas_call(
        _lower_bounds_kernel,
        out_shape=jax.ShapeDtypeStruct((two, n, 1, hk), F32),
        name="hgrn_lower_bounds",
    )(lb4)
    return out


def _mla_proj_kernel(x_ref, wa_ref, qn_ref, kvn_ref, wuq_ref, wuqr_ref, wuk_ref, wuv_ref,
                     cq_ref, sq_ref, ck_ref, sk_ref, q_ref, k_ref, v_ref):
    xb = x_ref[...].astype(BF16)
    a = _dot(xb, wa_ref[...])
    c_q = a[:, :MLA_Q_LORA]
    c_kv = a[:, MLA_Q_LORA:MLA_Q_LORA + MLA_KV_LORA]
    kr = a[:, 640:768]
    krr = a[:, 768:896]
    cqn = (c_q * lax.rsqrt(jnp.mean(c_q * c_q, axis=-1, keepdims=True) + RMS_EPS) * qn_ref[...]).astype(BF16)
    ckvn = (c_kv * lax.rsqrt(jnp.mean(c_kv * c_kv, axis=-1, keepdims=True) + RMS_EPS) * kvn_ref[...]).astype(BF16)
    q = _dot(cqn, wuq_ref[...])
    qr = _dot(cqn, wuqr_ref[...])
    cq = cq_ref[...]
    sq = sq_ref[...]
    kro = (kr * ck_ref[...] + krr * sk_ref[...]).astype(BF16)
    kn = _dot(ckvn, wuk_ref[...]).astype(BF16)
    v_ref[...] = _dot(ckvn, wuv_ref[...]).astype(BF16)
    for h in range(MLA_HEADS):
        lo = h * MLA_QK_PAD
        qh = (q[:, lo:lo + MLA_QK_PAD] * cq + qr[:, lo:lo + MLA_QK_PAD] * sq) * MLA_SCALE
        q_ref[:, lo:lo + MLA_QK_PAD] = qh.astype(BF16)
        k_ref[:, lo:lo + MLA_NOPE] = kn[:, h * MLA_NOPE:(h + 1) * MLA_NOPE]
        k_ref[:, lo + MLA_NOPE:lo + MLA_QK_PAD] = kro


def mla_proj(x2d, seq, w, tabs, tm=256):
    T = x2d.shape[0]
    tm = math.gcd(seq, tm)
    nper = seq // tm
    cq, sq, ck, sk = tabs
    row = lambda i: (i, 0)
    pos = lambda i: (i % nper, 0)
    hq = MLA_HEADS * MLA_QK_PAD
    return pl.pallas_call(
        _mla_proj_kernel,
        grid=(T // tm,),
        in_specs=[
            pl.BlockSpec((tm, D_MODEL), row),
            _full(w["wa"].shape), _full(w["qn"].shape), _full(w["kvn"].shape),
            _full(w["wuq"].shape), _full(w["wuqr"].shape), _full(w["wuk"].shape), _full(w["wuv"].shape),
            pl.BlockSpec((tm, MLA_QK_PAD), pos), pl.BlockSpec((tm, MLA_QK_PAD), pos),
            pl.BlockSpec((tm, LANES), pos), pl.BlockSpec((tm, LANES), pos),
        ],
        out_specs=[pl.BlockSpec((tm, hq), row), pl.BlockSpec((tm, hq), row),
                   pl.BlockSpec((tm, MLA_HEADS * MLA_V), row)],
        out_shape=[jax.ShapeDtypeStruct((T, hq), BF16), jax.ShapeDtypeStruct((T, hq), BF16),
                   jax.ShapeDtypeStruct((T, MLA_HEADS * MLA_V), BF16)],
        compiler_params=_params("parallel"),
        name="mla_proj",
    )(x2d, w["wa"], w["qn"], w["kvn"], w["wuq"], w["wuqr"], w["wuk"], w["wuv"], cq, sq, ck, sk)


def _attn_kernel(q_ref, k_ref, v_ref, o_ref, *, tk):
    q = q_ref[...]
    tq = q.shape[0]
    nk = k_ref.shape[0] // tk

    def body(c, carry):
        m, l, acc = carry
        off = pl.multiple_of(c * tk, tk)
        s = _dot_nt(q, k_ref[pl.ds(off, tk), :])
        m_new = jnp.maximum(m, jnp.max(s, axis=-1, keepdims=True))
        corr = jnp.exp(m - m_new)
        p = jnp.exp(s - m_new)
        l = corr * l + jnp.sum(p, axis=-1, keepdims=True)
        acc = corr * acc + _dot(p.astype(BF16), v_ref[pl.ds(off, tk), :])
        return m_new, l, acc

    m0 = jnp.full((tq, 1), -jnp.inf, F32)
    l0 = jnp.zeros((tq, 1), F32)
    a0 = jnp.zeros((tq, MLA_V), F32)
    _, l, acc = lax.fori_loop(0, nk, body, (m0, l0, a0))
    o_ref[...] = (acc / l).astype(o_ref.dtype)


def attention(q2d, k2d, v2d, batch, seq, tq=256, tk=512):
    T = q2d.shape[0]
    tq, tk = math.gcd(seq, tq), math.gcd(seq, tk)
    nq = seq // tq
    return pl.pallas_call(
        functools.partial(_attn_kernel, tk=tk),
        grid=(batch, MLA_HEADS, nq),
        in_specs=[
            pl.BlockSpec((tq, MLA_QK_PAD), lambda b, h, i: (b * nq + i, h)),
            pl.BlockSpec((seq, MLA_QK_PAD), lambda b, h, i: (b, h)),
            pl.BlockSpec((seq, MLA_V), lambda b, h, i: (b, h)),
        ],
        out_specs=pl.BlockSpec((tq, MLA_V), lambda b, h, i: (b * nq + i, h)),
        out_shape=jax.ShapeDtypeStruct((T, MLA_HEADS * MLA_V), BF16),
        compiler_params=_params("parallel", "parallel", "arbitrary"),
        name="mla_attention",
    )(q2d, k2d, v2d)


def _proj_res_ln_kernel(a_ref, x_ref, w_ref, g_ref, b_ref, o_ref, ob_ref, *, alpha):
    h = _dot(a_ref[...], w_ref[...])
    y = _layer_norm(alpha * x_ref[...] + h, g_ref[...], b_ref[...])
    o_ref[...] = y
    ob_ref[...] = y.astype(BF16)


def proj_res_ln(a2d, x2d, w, g, b, alpha, tm=512):
    T = x2d.shape[0]
    tm = math.gcd(T, tm)
    row = lambda i: (i, 0)
    return pl.pallas_call(
        functools.partial(_proj_res_ln_kernel, alpha=alpha),
        grid=(T // tm,),
        in_specs=[pl.BlockSpec((tm, a2d.shape[1]), row), pl.BlockSpec((tm, D_MODEL), row),
                  _full(w.shape), _full(g.shape), _full(b.shape)],
        out_specs=[pl.BlockSpec((tm, D_MODEL), row), pl.BlockSpec((tm, D_MODEL), row)],
        out_shape=[jax.ShapeDtypeStruct((T, D_MODEL), F32), jax.ShapeDtypeStruct((T, D_MODEL), BF16)],
        compiler_params=_params("parallel"),
        name="proj_res_ln",
    )(a2d, x2d, w, g, b)


def _matmul_kernel(x_ref, w_ref, o_ref):
    o_ref[...] = _dot(x_ref[...].astype(BF16), w_ref[...])


def matmul(x2d, w, tm=512, tn=1280):
    T, K = x2d.shape
    N = w.shape[1]
    tm = math.gcd(T, tm)
    return pl.pallas_call(
        _matmul_kernel,
        grid=(T // tm, N // tn),
        in_specs=[pl.BlockSpec((tm, K), lambda i, j: (i, 0)), pl.BlockSpec((K, tn), lambda i, j: (0, j))],
        out_specs=pl.BlockSpec((tm, tn), lambda i, j: (i, j)),
        out_shape=jax.ShapeDtypeStruct((T, N), F32),
        compiler_params=_params("parallel", "arbitrary"),
        name="matmul",
    )(x2d, w)


def _scan_tables():
    C, L = SCAN_CHUNK, SCAN_LEVELS
    mats = np.zeros((2, (2 + L) * C, C), np.float32)
    lvl = np.full((2, C, C), -1, np.int32)
    for d in range(2):
        p = np.arange(C) if d == 0 else C - 1 - np.arange(C)
        pt, pu = p[:, None], p[None, :]
        mats[d, 0:C] = pu <= pt
        mats[d, C:2 * C] = pu > pt
        for l in range(L):
            m = 1 << l
            r = (pt // (2 * m)) * (2 * m) + m - 1
            qside = (pt % (2 * m)) >= m
            e = np.where(qside, (pu > r) & (pu <= pt), (pu > pt) & (pu <= r))
            mats[d, (2 + l) * C:(3 + l) * C] = e
            same = (pt // (2 * m)) == (pu // (2 * m))
            lvl[d][same & qside & ((pu % (2 * m)) < m)] = l
        lvl[d][pt == pu] = L
    return jnp.asarray(mats, BF16), jnp.asarray(lvl)


def _hgrn_scan_kernel(q_ref, z_ref, v_ref, lb_ref, mat_ref, lvl_ref, o_ref, state_ref):
    C, L = SCAN_CHUNK, SCAN_LEVELS

    @pl.when(pl.program_id(2) == 0)
    def _():
        state_ref[...] = jnp.zeros_like(state_ref)

    mat = mat_ref[...]
    lvl = lvl_ref[...]
    for h in range(HGRN_HEADS):
        sl = slice(h * HGRN_DK, (h + 1) * HGRN_DK)
        q = _silu(q_ref[:, sl])
        z = z_ref[:, sl]
        lb = lb_ref[:, sl]
        e = jnp.exp(-jnp.abs(z))
        r = 1.0 / (1.0 + e)
        er = e * r
        pos = z >= 0
        f = lb + (1.0 - lb) * jnp.where(pos, r, er)
        g = jnp.log(jnp.maximum(f, GATE_FLOOR))
        kk = (1.0 - lb) * jnp.where(pos, er, r)
        g1 = g.astype(BF16)
        r1 = g - g1.astype(F32)
        g2 = r1.astype(BF16)
        g3 = (r1 - g2.astype(F32)).astype(BF16)
        ex = _dot(mat, g1) + _dot(mat, g2) + _dot(mat, g3)
        e_b = ex[0:C]
        e_last = ex[C:2 * C]
        b_last = e_b[0:1] + e_last[0:1]
        kb = kk.astype(BF16)
        attn = jnp.where(lvl == L, _dot_nt(q.astype(BF16), kb), 0.0)
        for l in range(L):
            xl = jnp.exp(ex[(2 + l) * C:(3 + l) * C])
            pl_ = _dot_nt((q * xl).astype(BF16), (kk * xl).astype(BF16))
            attn = jnp.where(lvl == l, pl_, attn)
        vb = v_ref[:, sl].astype(BF16)
        st = state_ref[h]
        o = _dot(attn.astype(BF16), vb) + _dot_nt((q * jnp.exp(e_b)).astype(BF16), st.astype(BF16))
        o_ref[:, sl] = o
        kd = (kk * jnp.exp(e_last)).astype(BF16)
        vt = v_ref[:, sl].T.astype(BF16)
        state_ref[h] = st * jnp.exp(b_last) + _dot(vt, kd)


def hgrn_scan(proj, lb, batch, seq):
    T = proj.shape[0]
    C = SCAN_CHUNK
    nc = seq // C
    hk = HGRN_HEADS * HGRN_DK
    mats, lvl = _scan_tables()

    def rowblk(b, d, c):
        return b * nc + c + d * (nc - 1 - 2 * c)

    return pl.pallas_call(
        _hgrn_scan_kernel,
        grid=(batch, 2, nc),
        in_specs=[
            pl.BlockSpec((C, hk), lambda b, d, c: (rowblk(b, d, c), 0)),
            pl.BlockSpec((C, hk), lambda b, d, c: (rowblk(b, d, c), 1 + d)),
            pl.BlockSpec((C, hk), lambda b, d, c: (rowblk(b, d, c), 3)),
            pl.BlockSpec((None, 1, hk), lambda b, d, c: (d, 0, 0)),
            pl.BlockSpec((None,) + mats.shape[1:], lambda b, d, c: (d, 0, 0)),
            pl.BlockSpec((None, C, C), lambda b, d, c: (d, 0, 0)),
        ],
        out_specs=pl.BlockSpec((None, C, hk), lambda b, d, c: (d, rowblk(b, d, c), 0)),
        out_shape=jax.ShapeDtypeStruct((2, T, hk), F32),
        scratch_shapes=[pltpu.VMEM((HGRN_HEADS, HGRN_DV, HGRN_DK), F32)],
        compiler_params=_params("parallel", "arbitrary", "arbitrary"),
        name="hgrn_scan",
    )(proj, proj, proj, lb, mats, lvl)


def _hgrn_out_kernel(o_ref, g_ref, x_ref, nw_ref, w_ref, lg_ref, lbias_ref, y_ref, yb_ref, a_ref, *, alpha):
    nw = nw_ref[...]
    for h in range(HGRN_HEADS):
        sl = slice(h * HGRN_DV, (h + 1) * HGRN_DV)
        o = o_ref[0, :, sl] + o_ref[1, :, sl]
        on = o * lax.rsqrt(jnp.mean(o * o, axis=-1, keepdims=True) + RMS_EPS) * nw
        a_ref[:, sl] = (on * _silu(g_ref[:, sl])).astype(BF16)
    hmix = _dot(a_ref[...], w_ref[...])
    y = _layer_norm(alpha * x_ref[...] + hmix, lg_ref[...], lbias_ref[...])
    y_ref[...] = y
    yb_ref[...] = y.astype(BF16)


def hgrn_out(o2, proj, x2d, nw, w, g, b, alpha, tm=256):
    T = x2d.shape[0]
    tm = math.gcd(T, tm)
    hv = HGRN_HEADS * HGRN_DV
    row = lambda i: (i, 0)
    return pl.pallas_call(
        functools.partial(_hgrn_out_kernel, alpha=alpha),
        grid=(T // tm,),
        in_specs=[pl.BlockSpec((2, tm, hv), lambda i: (0, i, 0)),
                  pl.BlockSpec((tm, hv), lambda i: (i, 4)),
                  pl.BlockSpec((tm, D_MODEL), row),
                  _full(nw.shape), _full(w.shape), _full(g.shape), _full(b.shape)],
        out_specs=[pl.BlockSpec((tm, D_MODEL), row), pl.BlockSpec((tm, D_MODEL), row)],
        out_shape=[jax.ShapeDtypeStruct((T, D_MODEL), F32), jax.ShapeDtypeStruct((T, D_MODEL), BF16)],
        scratch_shapes=[pltpu.VMEM((tm, hv), BF16)],
        compiler_params=_params("parallel"),
        name="hgrn_out",
    )(o2, proj, x2d, nw, w, g, b)


_PAIRS = [(a, b) for a in range(PEER_TOPK) for b in range(PEER_TOPK) if (a + 1) * (b + 1) <= PEER_TOPK]
_NPAIR_PAD = -(-len(_PAIRS) // 8) * 8


def _top_rows(s, k, payload=None):
    n = s.shape[0]
    iota = lax.broadcasted_iota(jnp.int32, s.shape, 0)
    vals, idxs = [], []
    for _ in range(k):
        m = jnp.max(s, axis=0, keepdims=True)
        im = jnp.min(jnp.where(s == m, iota, n), axis=0, keepdims=True)
        hit = iota == im
        vals.append(m)
        if payload is None:
            idxs.append(im)
        else:
            idxs.append(jnp.max(jnp.where(hit, payload, -1), axis=0, keepdims=True))
        s = jnp.where(hit, -jnp.inf, s)
    return vals, idxs


def _peer_topk_kernel(xb_ref, wq_ref, keys_ref, gate_ref, eid_ref, q_ref):
    tm = xb_ref.shape[0]
    q_ref[...] = _dot(xb_ref[...], wq_ref[...]).astype(BF16)
    half = PEER_DK // 2
    neg = jnp.full((_NPAIR_PAD - len(_PAIRS), tm), -jnp.inf, F32)
    zero = jnp.zeros((_NPAIR_PAD - len(_PAIRS), tm), jnp.int32)

    def head(h, carry):
        sv, si = [], []
        for c in range(2):
            off = pl.multiple_of(h * PEER_DK + c * half, half)
            s = _dot_nt(keys_ref[c], q_ref[:, pl.ds(off, half)])
            v_, i_ = _top_rows(s, PEER_TOPK)
            sv.append(v_)
            si.append(i_)
        cand = jnp.concatenate([sv[0][a] + sv[1][b] for a, b in _PAIRS] + [neg], axis=0)
        cid = jnp.concatenate([si[0][a] * N_KEYS + si[1][b] for a, b in _PAIRS] + [zero], axis=0)
        tv, te = _top_rows(cand, PEER_TOPK, payload=cid)
        tv = jnp.concatenate(tv, axis=0)
        ex = jnp.exp(tv - tv[0:1])
        gate = ex / jnp.sum(ex, axis=0, keepdims=True)
        row = pl.multiple_of(h * PEER_TOPK, PEER_TOPK)
        gate_ref[pl.ds(row, PEER_TOPK), :] = gate
        eid_ref[pl.ds(row, PEER_TOPK), :] = jnp.concatenate(te, axis=0)
        return carry

    lax.fori_loop(0, PEER_HEADS, head, 0)


def peer_topk(xb, wq, keys, tm=256):
    T = xb.shape[0]
    tm = math.gcd(T, tm)
    ns = PEER_HEADS * PEER_TOPK
    return pl.pallas_call(
        _peer_topk_kernel,
        grid=(T // tm,),
        in_specs=[pl.BlockSpec((tm, D_MODEL), lambda i: (i, 0)), _full(wq.shape), _full(keys.shape)],
        out_specs=[pl.BlockSpec((ns, tm), lambda i: (0, i)), pl.BlockSpec((ns, tm), lambda i: (0, i))],
        out_shape=[jax.ShapeDtypeStruct((ns, T), F32), jax.ShapeDtypeStruct((ns, T), jnp.int32)],
        scratch_shapes=[pltpu.VMEM((tm, PEER_HEADS * PEER_DK), BF16)],
        compiler_params=_params("parallel"),
        name="peer_topk",
    )(xb, wq, keys)


def _peer_w_kernel(gate_ref, eid_ref, w_ref, gt_ref, it_ref, jt_ref, tile_ref):
    tm = gate_ref.shape[1]
    eid_t = eid_ref[...].T
    gt_ref[...] = gate_ref[...].T
    it_ref[...] = lax.shift_right_logical(eid_t, int(math.log2(N_KEYS)))
    jt_ref[...] = lax.bitwise_and(eid_t, N_KEYS - 1)
    ns = gate_ref.shape[0]
    iota = lax.broadcasted_iota(jnp.int32, (N_KEYS, ns), 0)

    def token(t, carry):
        gi = gt_ref[pl.ds(t, 1), :]
        ii = it_ref[pl.ds(t, 1), :]
        jj = jt_ref[pl.ds(t, 1), :]
        at = jnp.where(iota == ii, gi, 0.0).astype(BF16)
        bt = jnp.where(iota == jj, 1.0, 0.0).astype(BF16)
        tile_ref[pl.ds(pl.multiple_of(t * W_PITCH, 8), N_KEYS), :] = _dot_nt(at, bt)
        return carry

    lax.fori_loop(0, tm, token, 0)

    def key_row(i, carry):
        col = tile_ref[pl.ds(i, tm, stride=W_PITCH), :]
        w_ref[:, pl.ds(pl.multiple_of(i * N_KEYS, N_KEYS), N_KEYS)] = col.astype(BF16)
        return carry

    lax.fori_loop(0, N_KEYS, key_row, 0)


def peer_w(gate, eid, tm=128):
    ns, T = gate.shape
    tm = math.gcd(T, tm)
    return pl.pallas_call(
        _peer_w_kernel,
        grid=(T // tm,),
        in_specs=[pl.BlockSpec((ns, tm), lambda i: (0, i)), pl.BlockSpec((ns, tm), lambda i: (0, i))],
        out_specs=pl.BlockSpec((tm, N_EXPERTS), lambda i: (i, 0)),
        out_shape=jax.ShapeDtypeStruct((T, N_EXPERTS), BF16),
        scratch_shapes=[pltpu.VMEM((tm, ns), F32), pltpu.VMEM((tm, ns), jnp.int32),
                        pltpu.VMEM((tm, ns), jnp.int32), pltpu.VMEM((tm * W_PITCH, N_KEYS), F32)],
        compiler_params=_params("parallel"),
        name="peer_gate_matrix",
    )(gate, eid)


def _peer_dense_kernel(xb_ref, ut_ref, v_ref, w_ref, o_ref):
    @pl.when(pl.program_id(1) == 0)
    def _():
        o_ref[...] = jnp.zeros_like(o_ref)

    h = _dot(xb_ref[...], ut_ref[...])
    act = 0.5 * h * (1.0 + lax.erf(h * (1.0 / math.sqrt(2.0))))
    hw = (act * w_ref[...].astype(F32)).astype(BF16)
    o_ref[...] += _dot(hw, v_ref[...])


def peer_dense(xb, ut, v, wmat, tm=512, ne=1024):
    T = xb.shape[0]
    tm = math.gcd(T, tm)
    return pl.pallas_call(
        _peer_dense_kernel,
        grid=(T // tm, N_EXPERTS // ne),
        in_specs=[pl.BlockSpec((tm, D_MODEL), lambda i, j: (i, 0)),
                  pl.BlockSpec((D_MODEL, ne), lambda i, j: (0, j)),
                  pl.BlockSpec((ne, D_MODEL), lambda i, j: (j, 0)),
                  pl.BlockSpec((tm, ne), lambda i, j: (i, j))],
        out_specs=pl.BlockSpec((tm, D_MODEL), lambda i, j: (i, 0)),
        out_shape=jax.ShapeDtypeStruct((T, D_MODEL), F32),
        compiler_params=_params("parallel", "arbitrary"),
        name="peer_dense",
    )(xb, ut, v, wmat)


def _ln_ple_kernel(x_ref, f_ref, p_ref, g_ref, b_ref, wg_ref, bg_ref, wp_ref, o_ref, *, alpha):
    y = _layer_norm(alpha * x_ref[...] + f_ref[...], g_ref[...], b_ref[...])
    gate = 1.0 / (1.0 + jnp.exp(-(_dot(y.astype(BF16), wg_ref[...]) + bg_ref[...])))
    o_ref[...] = y + gate * _dot(p_ref[...].astype(BF16), wp_ref[...])


def ln_ple(x2d, f2d, p2d, g, b, wg, bg, wp, alpha, tm=512):
    T = x2d.shape[0]
    tm = math.gcd(T, tm)
    row = lambda i: (i, 0)
    return pl.pallas_call(
        functools.partial(_ln_ple_kernel, alpha=alpha),
        grid=(T // tm,),
        in_specs=[pl.BlockSpec((tm, D_MODEL), row), pl.BlockSpec((tm, D_MODEL), row),
                  pl.BlockSpec((tm, PLE_DIM), row),
                  _full(g.shape), _full(b.shape), _full(wg.shape), _full(bg.shape), _full(wp.shape)],
        out_specs=pl.BlockSpec((tm, D_MODEL), row),
        out_shape=jax.ShapeDtypeStruct((T, D_MODEL), F32),
        compiler_params=_params("parallel"),
        name="ln_ple",
    )(x2d, f2d, p2d, g, b, wg, bg, wp)


def _rot_cols(w):
    half = MLA_ROPE // 2
    return jnp.concatenate([-w[..., half:], w[..., :half]], axis=-1)


def _mla_weights(w_a, q_norm, kv_norm, w_uq, w_ukv):
    zk = jnp.zeros((D_MODEL, LANES - MLA_ROPE), F32)
    w_kr = w_a[:, MLA_Q_LORA + MLA_KV_LORA:]
    wa = jnp.concatenate([w_a[:, :MLA_Q_LORA + MLA_KV_LORA], w_kr, zk, _rot_cols(w_kr), zk], axis=1)
    uq = w_uq.reshape(MLA_Q_LORA, MLA_HEADS, MLA_NOPE + MLA_ROPE)
    zq = jnp.zeros((MLA_Q_LORA, MLA_HEADS, MLA_QK_PAD - MLA_NOPE - MLA_ROPE), F32)
    wuq = jnp.concatenate([uq, zq], axis=-1).reshape(MLA_Q_LORA, MLA_HEADS * MLA_QK_PAD)
    wuqr = jnp.concatenate([jnp.zeros_like(uq[..., :MLA_NOPE]), _rot_cols(uq[..., MLA_NOPE:]), zq],
                           axis=-1).reshape(MLA_Q_LORA, MLA_HEADS * MLA_QK_PAD)
    ukv = w_ukv.reshape(MLA_KV_LORA, MLA_HEADS, MLA_NOPE + MLA_V)
    return {
        "wa": wa.astype(BF16),
        "qn": q_norm.astype(F32).reshape(1, -1),
        "kvn": kv_norm.astype(F32).reshape(1, -1),
        "wuq": wuq.astype(BF16),
        "wuqr": wuqr.astype(BF16),
        "wuk": ukv[..., :MLA_NOPE].reshape(MLA_KV_LORA, -1).astype(BF16),
        "wuv": ukv[..., MLA_NOPE:].reshape(MLA_KV_LORA, -1).astype(BF16),
    }


def _rope_tabs(seq):
    inv_freq = 1.0 / (ROPE_THETA ** (jnp.arange(0, MLA_ROPE, 2, dtype=F32) / MLA_ROPE))
    ang = jnp.arange(seq, dtype=F32)[:, None] * inv_freq[None, :]
    emb = jnp.concatenate([ang, ang], axis=-1)
    cos, sin = jnp.cos(emb), jnp.sin(emb)
    one = jnp.ones((seq, MLA_NOPE), F32)
    z64 = jnp.zeros((seq, LANES - MLA_ROPE), F32)
    z128 = jnp.zeros((seq, MLA_NOPE), F32)
    return (jnp.concatenate([one, cos, z64], axis=1), jnp.concatenate([z128, sin, z64], axis=1),
            jnp.concatenate([cos, z64], axis=1), jnp.concatenate([sin, z64], axis=1))


def _trunk(x, p, lbs, mla_w, hgrn_w, peer_w_, ln_w, ple_w, depth):
    batch, seq, _ = x.shape
    T = batch * seq
    alpha = (2 * depth) ** 0.25
    x2d = x.reshape(T, D_MODEL)
    tabs = _rope_tabs(seq)
    for i in range(depth):
        j = i // 2
        g1, b1, g2, b2 = ln_w[i]
        if i % 2 == 0:
            w = mla_w[j]
            q2d, k2d, v2d = mla_proj(x2d, seq, w, tabs)
            o2d = attention(q2d, k2d, v2d, batch, seq)
            x1, x1b = proj_res_ln(o2d, x2d, w["wo"], g1, b1, alpha)
        else:
            w = hgrn_w[j]
            proj = matmul(x2d, w["win"])
            o2 = hgrn_scan(proj, lbs[:, j], batch, seq)
            x1, x1b = hgrn_out(o2, proj, x2d, w["nw"], w["wo"], g1, b1, alpha)
        pw = peer_w_[i]
        gate, eid = peer_topk(x1b, pw["wq"], pw["keys"])
        wmat = peer_w(gate, eid)
        f2d = peer_dense(x1b, pw["ut"], pw["v"], wmat)
        wg, bg, wp = ple_w[i]
        x2d = ln_ple(x1, f2d, p[i].reshape(T, PLE_DIM), g2, b2, wg, bg, wp, alpha)
    return x2d.reshape(batch, seq, D_MODEL)


def kernel(x_prompt, x_sample, p_prompt, p_sample, mla_w_a, mla_q_norm, mla_kv_norm, mla_w_uq, mla_w_ukv, mla_w_o, hgrn_w_in, hgrn_lb, hgrn_norm, hgrn_w_o, peer_w_q, peer_sub_keys, peer_u, peer_v, ln1_g, ln1_b, ln2_g, ln2_b, ple_gate_w, ple_gate_b, ple_proj):
    depth = peer_w_q.shape[0]
    lbs = lower_bounds(hgrn_lb)
    mla_w = []
    for j in range(mla_w_a.shape[0]):
        w = _mla_weights(mla_w_a[j], mla_q_norm[j], mla_kv_norm[j], mla_w_uq[j], mla_w_ukv[j])
        w["wo"] = mla_w_o[j].astype(BF16)
        mla_w.append(w)
    hgrn_w = [{"win": hgrn_w_in[j].astype(BF16), "nw": hgrn_norm[j].astype(F32).reshape(1, -1),
               "wo": hgrn_w_o[j].astype(BF16)} for j in range(hgrn_w_in.shape[0])]
    peer_w_ = [{"wq": peer_w_q[i].astype(BF16), "keys": peer_sub_keys[i].astype(BF16),
                "ut": peer_u[i].astype(BF16).T, "v": peer_v[i].astype(BF16)} for i in range(depth)]
    r = lambda a: a.astype(F32).reshape(1, -1)
    ln_w = [(r(ln1_g[i]), r(ln1_b[i]), r(ln2_g[i]), r(ln2_b[i])) for i in range(depth)]
    ple_w = [(ple_gate_w[i].astype(BF16), r(ple_gate_b[i]), ple_proj[i].astype(BF16)) for i in range(depth)]
    args = (lbs, mla_w, hgrn_w, peer_w_, ln_w, ple_w, depth)
    return (_trunk(x_prompt, p_prompt, *args), _trunk(x_sample, p_sample, *args))
```

```python
import functools
import math

import numpy as np
import jax
import jax.numpy as jnp
from jax import lax
from jax.experimental import pallas as pl
from jax.experimental.pallas import tpu as pltpu

D_MODEL = 1024
PLE_DIM = 256
MLA_HEADS = 8
MLA_NOPE = 128
MLA_ROPE = 64
MLA_V = 128
MLA_Q_LORA = 384
MLA_KV_LORA = 256
MLA_SCALE = (MLA_NOPE + MLA_ROPE) ** -0.5
MLA_QK_PAD = 256
LOG2E = math.log2(math.e)
ROPE_THETA = 10000.0
HGRN_HEADS = 8
HGRN_DK = 128
HGRN_DV = 128
GATE_FLOOR = 1e-30
PEER_HEADS = 8
PEER_DK = 256
N_KEYS = 128
N_EXPERTS = N_KEYS * N_KEYS
PEER_TOPK = 16
LN_EPS = 1e-5
RMS_EPS = 1e-6

LANES = 128
VMEM_LIMIT = 56 * 1024 * 1024

SCAN_CHUNK = 128
SCAN_LEVELS = 7
W_PITCH = 136
W_GROUP = 16
W_ROWS = 8

F32 = jnp.float32
BF16 = jnp.bfloat16

_NT = (((1,), (1,)), ((), ()))


def _params(*sem):
    return pltpu.CompilerParams(dimension_semantics=sem, vmem_limit_bytes=VMEM_LIMIT)


def _dot(a, b):
    return jnp.dot(a, b, preferred_element_type=F32)


def _dot_nt(a, b):
    return lax.dot_general(a, b, _NT, preferred_element_type=F32)


def _layer_norm(y, g, b):
    mu = jnp.mean(y, axis=-1, keepdims=True)
    yc = y - mu
    var = jnp.mean(yc * yc, axis=-1, keepdims=True)
    return yc * lax.rsqrt(var + LN_EPS) * g + b


def _silu(t):
    return t * (1.0 / (1.0 + jnp.exp(-t)))


def _full(shape):
    nd = len(shape)
    return pl.BlockSpec(shape, lambda *_: (0,) * nd)


def _lower_bounds_kernel(lb_ref, out_ref):
    n = lb_ref.shape[1]
    for d in range(lb_ref.shape[0]):
        rows = [lb_ref[d, j] for j in range(n)]
        m = rows[0]
        for r in rows[1:]:
            m = jnp.maximum(m, r)
        ex = [jnp.exp(r - m) for r in rows]
        tot = ex[0]
        for e in ex[1:]:
            tot = tot + e
        sm = [e / tot for e in ex]
        cum = sm[0]
        out_ref[d, 0] = cum - sm[0]
        for j in range(1, n):
            cum = cum + sm[j]
            out_ref[d, j] = cum - sm[0]


def lower_bounds(hgrn_lb):
    two, n, hk = hgrn_lb.shape
    lb4 = hgrn_lb.astype(F32).reshape(two, n, 1, hk)
    out = pl.pallas_call(
        _lower_bounds_kernel,
        out_shape=jax.ShapeDtypeStruct((two, n, 1, hk), F32),
        name="hgrn_lower_bounds",
    )(lb4)
    return out


def _mla_proj_kernel(x_ref, wa_ref, qn_ref, kvn_ref, wuq_ref, wuqr_ref, wuk_ref, wuv_ref,
                     cq_ref, sq_ref, ck_ref, sk_ref, q_ref, k_ref, v_ref):
    xb = x_ref[...].astype(BF16)
    a = _dot(xb, wa_ref[...])
    c_q = a[:, :MLA_Q_LORA]
    c_kv = a[:, MLA_Q_LORA:MLA_Q_LORA + MLA_KV_LORA]
    kr = a[:, 640:768]
    krr = a[:, 768:896]
    cqn = (c_q * lax.rsqrt(jnp.mean(c_q * c_q, axis=-1, keepdims=True) + RMS_EPS) * qn_ref[...]).astype(BF16)
    ckvn = (c_kv * lax.rsqrt(jnp.mean(c_kv * c_kv, axis=-1, keepdims=True) + RMS_EPS) * kvn_ref[...]).astype(BF16)
    q = _dot(cqn, wuq_ref[...])
    qr = _dot(cqn, wuqr_ref[...])
    cq = cq_ref[...]
    sq = sq_ref[...]
    kro = (kr * ck_ref[...] + krr * sk_ref[...]).astype(BF16)
    kn = _dot(ckvn, wuk_ref[...]).astype(BF16)
    v_ref[...] = _dot(ckvn, wuv_ref[...]).astype(BF16)
    for h in range(MLA_HEADS):
        lo = h * MLA_QK_PAD
        qh = (q[:, lo:lo + MLA_QK_PAD] * cq + qr[:, lo:lo + MLA_QK_PAD] * sq) * (MLA_SCALE * LOG2E)
        q_ref[:, lo:lo + MLA_QK_PAD] = qh.astype(BF16)
        k_ref[:, lo:lo + MLA_NOPE] = kn[:, h * MLA_NOPE:(h + 1) * MLA_NOPE]
        k_ref[:, lo + MLA_NOPE:lo + MLA_QK_PAD] = kro


def mla_proj(x2d, seq, w, tabs, tm=256):
    T = x2d.shape[0]
    tm = math.gcd(seq, tm)
    nper = seq // tm
    cq, sq, ck, sk = tabs
    row = lambda i: (i, 0)
    pos = lambda i: (i % nper, 0)
    hq = MLA_HEADS * MLA_QK_PAD
    return pl.pallas_call(
        _mla_proj_kernel,
        grid=(T // tm,),
        in_specs=[
            pl.BlockSpec((tm, D_MODEL), row),
            _full(w["wa"].shape), _full(w["qn"].shape), _full(w["kvn"].shape),
            _full(w["wuq"].shape), _full(w["wuqr"].shape), _full(w["wuk"].shape), _full(w["wuv"].shape),
            pl.BlockSpec((tm, MLA_QK_PAD), pos), pl.BlockSpec((tm, MLA_QK_PAD), pos),
            pl.BlockSpec((tm, LANES), pos), pl.BlockSpec((tm, LANES), pos),
        ],
        out_specs=[pl.BlockSpec((tm, hq), row), pl.BlockSpec((tm, hq), row),
                   pl.BlockSpec((tm, MLA_HEADS * MLA_V), row)],
        out_shape=[jax.ShapeDtypeStruct((T, hq), BF16), jax.ShapeDtypeStruct((T, hq), BF16),
                   jax.ShapeDtypeStruct((T, MLA_HEADS * MLA_V), BF16)],
        compiler_params=_params("parallel"),
        name="mla_proj",
    )(x2d, w["wa"], w["qn"], w["kvn"], w["wuq"], w["wuqr"], w["wuk"], w["wuv"], cq, sq, ck, sk)


def _attn_kernel(q_ref, kt_ref, v_ref, o_ref, *, tk, unroll):
    q = q_ref[...]
    tq = q.shape[0]
    nk = kt_ref.shape[1] // tk

    def body(c, carry):
        m, l, acc = carry
        for u in range(unroll):
            off = pl.multiple_of((c * unroll + u) * tk, tk)
            s = _dot(q, kt_ref[:, pl.ds(off, tk)])
            m_new = jnp.maximum(m, jnp.max(s, axis=-1, keepdims=True))
            corr = jnp.exp2(m - m_new)
            p = jnp.exp2(s - m_new)
            l = corr * l + jnp.sum(p, axis=-1, keepdims=True)
            acc = corr * acc + _dot(p.astype(BF16), v_ref[pl.ds(off, tk), :])
            m = m_new
        return m, l, acc

    init = (jnp.full((tq, 1), -jnp.inf, F32), jnp.zeros((tq, 1), F32), jnp.zeros((tq, MLA_V), F32))
    _, l, acc = lax.fori_loop(0, nk // unroll, body, init)
    o_ref[...] = (acc / l).astype(o_ref.dtype)


def attention(q2d, kt2d, v2d, batch, seq, tq=512, tk=512, unroll=4):
    T = q2d.shape[0]
    tq, tk = math.gcd(seq, tq), math.gcd(seq, tk)
    unroll = math.gcd(seq // tk, unroll)
    nq = seq // tq
    return pl.pallas_call(
        functools.partial(_attn_kernel, tk=tk, unroll=unroll),
        grid=(batch, MLA_HEADS, nq),
        in_specs=[
            pl.BlockSpec((tq, MLA_QK_PAD), lambda b, h, i: (b * nq + i, h)),
            pl.BlockSpec((MLA_QK_PAD, seq), lambda b, h, i: (h, b)),
            pl.BlockSpec((seq, MLA_V), lambda b, h, i: (b, h)),
        ],
        out_specs=pl.BlockSpec((tq, MLA_V), lambda b, h, i: (b * nq + i, h)),
        out_shape=jax.ShapeDtypeStruct((T, MLA_HEADS * MLA_V), BF16),
        compiler_params=_params("parallel", "parallel", "arbitrary"),
        name="mla_attention",
    )(q2d, kt2d, v2d)


def _proj_res_ln_kernel(a_ref, x_ref, w_ref, g_ref, b_ref, o_ref, ob_ref, *, alpha):
    h = _dot(a_ref[...], w_ref[...])
    y = _layer_norm(alpha * x_ref[...] + h, g_ref[...], b_ref[...])
    o_ref[...] = y
    ob_ref[...] = y.astype(BF16)


def proj_res_ln(a2d, x2d, w, g, b, alpha, tm=512):
    T = x2d.shape[0]
    tm = math.gcd(T, tm)
    row = lambda i: (i, 0)
    return pl.pallas_call(
        functools.partial(_proj_res_ln_kernel, alpha=alpha),
        grid=(T // tm,),
        in_specs=[pl.BlockSpec((tm, a2d.shape[1]), row), pl.BlockSpec((tm, D_MODEL), row),
                  _full(w.shape), _full(g.shape), _full(b.shape)],
        out_specs=[pl.BlockSpec((tm, D_MODEL), row), pl.BlockSpec((tm, D_MODEL), row)],
        out_shape=[jax.ShapeDtypeStruct((T, D_MODEL), F32), jax.ShapeDtypeStruct((T, D_MODEL), BF16)],
        compiler_params=_params("parallel"),
        name="proj_res_ln",
    )(a2d, x2d, w, g, b)


def _matmul_kernel(x_ref, w_ref, o_ref):
    o_ref[...] = _dot(x_ref[...].astype(BF16), w_ref[...])


def matmul(x2d, w, tm=512, tn=1280):
    T, K = x2d.shape
    N = w.shape[1]
    tm = math.gcd(T, tm)
    return pl.pallas_call(
        _matmul_kernel,
        grid=(T // tm, N // tn),
        in_specs=[pl.BlockSpec((tm, K), lambda i, j: (i, 0)), pl.BlockSpec((K, tn), lambda i, j: (0, j))],
        out_specs=pl.BlockSpec((tm, tn), lambda i, j: (i, j)),
        out_shape=jax.ShapeDtypeStruct((T, N), F32),
        compiler_params=_params("parallel", "arbitrary"),
        name="matmul",
    )(x2d, w)


def _scan_tables():
    C, L = SCAN_CHUNK, SCAN_LEVELS
    mats = np.zeros((2, (2 + L) * C, C), np.float32)
    lvl = np.full((2, C, C), -1, np.int32)
    for d in range(2):
        p = np.arange(C) if d == 0 else C - 1 - np.arange(C)
        pt, pu = p[:, None], p[None, :]
        mats[d, 0:C] = pu <= pt
        mats[d, C:2 * C] = pu > pt
        for l in range(L):
            m = 1 << l
            r = (pt // (2 * m)) * (2 * m) + m - 1
            qside = (pt % (2 * m)) >= m
            e = np.where(qside, (pu > r) & (pu <= pt), (pu > pt) & (pu <= r))
            mats[d, (2 + l) * C:(3 + l) * C] = e
            same = (pt // (2 * m)) == (pu // (2 * m))
            lvl[d][same & qside & ((pu % (2 * m)) < m)] = l
        lvl[d][pt == pu] = L
    return jnp.asarray(np.concatenate([mats, mats], axis=2), BF16), jnp.asarray(np.concatenate([lvl, lvl], axis=2))


def _hgrn_scan_kernel(q_ref, z_ref, v_ref, lb_ref, mat_ref, lvl_ref, o_ref, state_ref, kk_ref, ex_ref):
    C, L = SCAN_CHUNK, SCAN_LEVELS

    @pl.when(pl.program_id(2) == 0)
    def _():
        state_ref[...] = jnp.zeros_like(state_ref)

    z = z_ref[...]
    lb = lb_ref[...]
    e = jnp.exp(-jnp.abs(z))
    r = 1.0 / (1.0 + e)
    er = e * r
    pos = z >= 0
    f = lb + (1.0 - lb) * jnp.where(pos, r, er)
    g = jnp.log(jnp.maximum(f, GATE_FLOOR))
    kk_ref[...] = (1.0 - lb) * jnp.where(pos, er, r)
    g1 = g.astype(BF16)
    g2 = (g - g1.astype(F32)).astype(BF16)
    ex_ref[...] = _dot(mat_ref[...], jnp.concatenate([g1, g2], axis=0))

    lvl = lvl_ref[...]
    zero = jnp.zeros((C, HGRN_DK), BF16)
    wide = 2 * HGRN_DK

    def block_diag(x):
        return jnp.concatenate([jnp.concatenate([x[:, :HGRN_DK], zero], axis=1),
                                jnp.concatenate([zero, x[:, HGRN_DK:]], axis=1)], axis=0)

    for hp in range(HGRN_HEADS // 2):
        sl = slice(hp * wide, (hp + 1) * wide)
        q = _silu(q_ref[:, sl])
        kk = kk_ref[:, sl]
        attn = jnp.where(lvl == L, _dot_nt(q.astype(BF16), block_diag(kk.astype(BF16))), 0.0)
        for l in range(L):
            xl = jnp.exp(ex_ref[(2 + l) * C:(3 + l) * C, sl])
            p = _dot_nt((q * xl).astype(BF16), block_diag((kk * xl).astype(BF16)))
            attn = jnp.where(lvl == l, p, attn)
        v2 = v_ref[:, sl]
        e_b = ex_ref[0:C, sl]
        e_last = ex_ref[C:2 * C, sl]
        b_last = e_b[0:1] + e_last[0:1]
        st = [state_ref[2 * hp + d] for d in range(2)]
        st_bd = jnp.concatenate([jnp.concatenate([st[0].astype(BF16), zero], axis=1),
                                 jnp.concatenate([zero, st[1].astype(BF16)], axis=1)], axis=0)
        o = _dot(attn.astype(BF16), block_diag(v2.astype(BF16)))
        o_ref[:, sl] = o + _dot_nt((q * jnp.exp(e_b)).astype(BF16), st_bd)
        kd = (kk * jnp.exp(e_last)).astype(BF16)
        for d in range(2):
            hs = slice(d * HGRN_DK, (d + 1) * HGRN_DK)
            vt = v2[:, hs].T.astype(BF16)
            state_ref[2 * hp + d] = st[d] * jnp.exp(b_last[:, hs]) + _dot(vt, kd[:, hs])


def hgrn_scan(proj, lb, batch, seq):
    T = proj.shape[0]
    C = SCAN_CHUNK
    nc = seq // C
    hk = HGRN_HEADS * HGRN_DK
    mats, lvl = _scan_tables()

    def rowblk(b, d, c):
        return b * nc + c + d * (nc - 1 - 2 * c)

    return pl.pallas_call(
        _hgrn_scan_kernel,
        grid=(batch, 2, nc),
        in_specs=[
            pl.BlockSpec((C, hk), lambda b, d, c: (rowblk(b, d, c), 0)),
            pl.BlockSpec((C, hk), lambda b, d, c: (rowblk(b, d, c), 1 + d)),
            pl.BlockSpec((C, hk), lambda b, d, c: (rowblk(b, d, c), 3)),
            pl.BlockSpec((None, 1, hk), lambda b, d, c: (d, 0, 0)),
            pl.BlockSpec((None,) + mats.shape[1:], lambda b, d, c: (d, 0, 0)),
            pl.BlockSpec((None, C, 2 * C), lambda b, d, c: (d, 0, 0)),
        ],
        out_specs=pl.BlockSpec((None, C, hk), lambda b, d, c: (d, rowblk(b, d, c), 0)),
        out_shape=jax.ShapeDtypeStruct((2, T, hk), F32),
        scratch_shapes=[pltpu.VMEM((HGRN_HEADS, HGRN_DV, HGRN_DK), F32), pltpu.VMEM((C, hk), F32),
                        pltpu.VMEM(((2 + SCAN_LEVELS) * C, hk), F32)],
        compiler_params=_params("parallel", "arbitrary", "arbitrary"),
        name="hgrn_scan",
    )(proj, proj, proj, lb, mats, lvl)


def _hgrn_out_kernel(o_ref, g_ref, x_ref, nw_ref, w_ref, lg_ref, lbias_ref, y_ref, yb_ref, a_ref, *, alpha):
    nw = nw_ref[...]
    for h in range(HGRN_HEADS):
        sl = slice(h * HGRN_DV, (h + 1) * HGRN_DV)
        o = o_ref[0, :, sl] + o_ref[1, :, sl]
        on = o * lax.rsqrt(jnp.mean(o * o, axis=-1, keepdims=True) + RMS_EPS) * nw
        a_ref[:, sl] = (on * _silu(g_ref[:, sl])).astype(BF16)
    hmix = _dot(a_ref[...], w_ref[...])
    y = _layer_norm(alpha * x_ref[...] + hmix, lg_ref[...], lbias_ref[...])
    y_ref[...] = y
    yb_ref[...] = y.astype(BF16)


def hgrn_out(o2, proj, x2d, nw, w, g, b, alpha, tm=256):
    T = x2d.shape[0]
    tm = math.gcd(T, tm)
    hv = HGRN_HEADS * HGRN_DV
    row = lambda i: (i, 0)
    return pl.pallas_call(
        functools.partial(_hgrn_out_kernel, alpha=alpha),
        grid=(T // tm,),
        in_specs=[pl.BlockSpec((2, tm, hv), lambda i: (0, i, 0)),
                  pl.BlockSpec((tm, hv), lambda i: (i, 4)),
                  pl.BlockSpec((tm, D_MODEL), row),
                  _full(nw.shape), _full(w.shape), _full(g.shape), _full(b.shape)],
        out_specs=[pl.BlockSpec((tm, D_MODEL), row), pl.BlockSpec((tm, D_MODEL), row)],
        out_shape=[jax.ShapeDtypeStruct((T, D_MODEL), F32), jax.ShapeDtypeStruct((T, D_MODEL), BF16)],
        scratch_shapes=[pltpu.VMEM((tm, hv), BF16)],
        compiler_params=_params("parallel"),
        name="hgrn_out",
    )(o2, proj, x2d, nw, w, g, b)


_PAIRS = [(a, b) for a in range(PEER_TOPK) for b in range(PEER_TOPK) if (a + 1) * (b + 1) <= PEER_TOPK]
_NPAIR_PAD = -(-len(_PAIRS) // 8) * 8


def _top_rows(s, k, payload=None):
    n = s.shape[0]
    iota = lax.broadcasted_iota(jnp.int32, s.shape, 0)
    vals, idxs = [], []
    for _ in range(k):
        m = jnp.max(s, axis=0, keepdims=True)
        im = jnp.min(jnp.where(s == m, iota, n), axis=0, keepdims=True)
        hit = iota == im
        vals.append(m)
        if payload is None:
            idxs.append(im)
        else:
            idxs.append(jnp.max(jnp.where(hit, payload, -1), axis=0, keepdims=True))
        s = jnp.where(hit, -jnp.inf, s)
    return vals, idxs


def _peer_topk_kernel(xb_ref, wq_ref, keys_ref, gate_ref, eid_ref, q_ref):
    tm = xb_ref.shape[0]
    q_ref[...] = _dot(xb_ref[...], wq_ref[...]).astype(BF16)
    half = PEER_DK // 2
    neg = jnp.full((_NPAIR_PAD - len(_PAIRS), tm), -jnp.inf, F32)
    zero = jnp.zeros((_NPAIR_PAD - len(_PAIRS), tm), jnp.int32)

    def head(h, carry):
        sv, si = [], []
        for c in range(2):
            off = pl.multiple_of(h * PEER_DK + c * half, half)
            s = _dot_nt(keys_ref[c], q_ref[:, pl.ds(off, half)])
            v_, i_ = _top_rows(s, PEER_TOPK)
            sv.append(v_)
            si.append(i_)
        cand = jnp.concatenate([sv[0][a] + sv[1][b] for a, b in _PAIRS] + [neg], axis=0)
        cid = jnp.concatenate([si[0][a] * N_KEYS + si[1][b] for a, b in _PAIRS] + [zero], axis=0)
        tv, te = _top_rows(cand, PEER_TOPK, payload=cid)
        tv = jnp.concatenate(tv, axis=0)
        ex = jnp.exp(tv - tv[0:1])
        gate = ex / jnp.sum(ex, axis=0, keepdims=True)
        row = pl.multiple_of(h * PEER_TOPK, PEER_TOPK)
        gate_ref[pl.ds(row, PEER_TOPK), :] = gate
        eid_ref[pl.ds(row, PEER_TOPK), :] = jnp.concatenate(te, axis=0)
        return carry

    lax.fori_loop(0, PEER_HEADS, head, 0)


def peer_topk(xb, wq, keys, tm=256):
    T = xb.shape[0]
    tm = math.gcd(T, tm)
    ns = PEER_HEADS * PEER_TOPK
    return pl.pallas_call(
        _peer_topk_kernel,
        grid=(T // tm,),
        in_specs=[pl.BlockSpec((tm, D_MODEL), lambda i: (i, 0)), _full(wq.shape), _full(keys.shape)],
        out_specs=[pl.BlockSpec((ns, tm), lambda i: (0, i)), pl.BlockSpec((ns, tm), lambda i: (0, i))],
        out_shape=[jax.ShapeDtypeStruct((ns, T), F32), jax.ShapeDtypeStruct((ns, T), jnp.int32)],
        scratch_shapes=[pltpu.VMEM((tm, PEER_HEADS * PEER_DK), BF16)],
        compiler_params=_params("parallel"),
        name="peer_topk",
    )(xb, wq, keys)


def _peer_w_kernel(gate_ref, eid_ref, w_ref, gt_ref, it_ref, jt_ref, tile_ref):
    tm = gate_ref.shape[1]
    eid_t = eid_ref[...].T
    gt_ref[...] = gate_ref[...].T
    it_ref[...] = lax.shift_right_logical(eid_t, int(math.log2(N_KEYS)))
    jt_ref[...] = lax.bitwise_and(eid_t, N_KEYS - 1)
    ns = gate_ref.shape[0]
    iota = lax.broadcasted_iota(jnp.int32, (N_KEYS, ns), 0)

    zero = jnp.zeros((N_KEYS, ns), BF16)

    def token_group(p, carry):
        base = pl.multiple_of(p * W_GROUP, W_GROUP)
        g8 = gt_ref[pl.ds(base, W_GROUP), :]
        i8 = it_ref[pl.ds(base, W_GROUP), :]
        j8 = jt_ref[pl.ds(base, W_GROUP), :]
        for u in range(0, W_GROUP, 2):
            at = [jnp.where(iota == i8[u + d:u + d + 1], g8[u + d:u + d + 1], 0.0).astype(BF16) for d in range(2)]
            bt = [jnp.where(iota == j8[u + d:u + d + 1], 1.0, 0.0).astype(BF16) for d in range(2)]
            lhs = jnp.concatenate(at, axis=1)
            rhs = jnp.concatenate([jnp.concatenate([bt[0], zero], axis=1),
                                   jnp.concatenate([zero, bt[1]], axis=1)], axis=0)
            res = _dot_nt(lhs, rhs)
            for d in range(2):
                row = pl.multiple_of((base + u + d) * W_PITCH, 8)
                tile_ref[pl.ds(row, N_KEYS), :] = res[:, d * N_KEYS:(d + 1) * N_KEYS]
        return carry

    lax.fori_loop(0, tm // W_GROUP, token_group, 0)

    def key_rows(i2, carry):
        for d in range(W_ROWS):
            i = i2 * W_ROWS + d
            col = tile_ref[pl.ds(i, tm, stride=W_PITCH), :]
            w_ref[:, pl.ds(pl.multiple_of(i * N_KEYS, N_KEYS), N_KEYS)] = col.astype(BF16)
        return carry

    lax.fori_loop(0, N_KEYS // W_ROWS, key_rows, 0)


def peer_w(gate, eid, tm=128):
    ns, T = gate.shape
    tm = math.gcd(T, tm)
    return pl.pallas_call(
        _peer_w_kernel,
        grid=(T // tm,),
        in_specs=[pl.BlockSpec((ns, tm), lambda i: (0, i)), pl.BlockSpec((ns, tm), lambda i: (0, i))],
        out_specs=pl.BlockSpec((tm, N_EXPERTS), lambda i: (i, 0)),
        out_shape=jax.ShapeDtypeStruct((T, N_EXPERTS), BF16),
        scratch_shapes=[pltpu.VMEM((tm, ns), F32), pltpu.VMEM((tm, ns), jnp.int32),
                        pltpu.VMEM((tm, ns), jnp.int32), pltpu.VMEM((tm * W_PITCH, N_KEYS), F32)],
        compiler_params=_params("parallel"),
        name="peer_gate_matrix",
    )(gate, eid)


def _peer_dense_kernel(xb_ref, ut_ref, v_ref, w_ref, o_ref):
    @pl.when(pl.program_id(1) == 0)
    def _():
        o_ref[...] = jnp.zeros_like(o_ref)

    h = _dot(xb_ref[...], ut_ref[...])
    act = 0.5 * h * (1.0 + lax.erf(h * (1.0 / math.sqrt(2.0))))
    hw = (act * w_ref[...].astype(F32)).astype(BF16)
    o_ref[...] += _dot(hw, v_ref[...])


def peer_dense(xb, ut, v, wmat, tm=512, ne=1024):
    T = xb.shape[0]
    tm = math.gcd(T, tm)
    return pl.pallas_call(
        _peer_dense_kernel,
        grid=(T // tm, N_EXPERTS // ne),
        in_specs=[pl.BlockSpec((tm, D_MODEL), lambda i, j: (i, 0)),
                  pl.BlockSpec((D_MODEL, ne), lambda i, j: (0, j)),
                  pl.BlockSpec((ne, D_MODEL), lambda i, j: (j, 0)),
                  pl.BlockSpec((tm, ne), lambda i, j: (i, j))],
        out_specs=pl.BlockSpec((tm, D_MODEL), lambda i, j: (i, 0)),
        out_shape=jax.ShapeDtypeStruct((T, D_MODEL), F32),
        compiler_params=_params("parallel", "arbitrary"),
        name="peer_dense",
    )(xb, ut, v, wmat)


def _ln_ple_kernel(x_ref, f_ref, p_ref, g_ref, b_ref, wg_ref, bg_ref, wp_ref, o_ref, *, alpha):
    y = _layer_norm(alpha * x_ref[...] + f_ref[...], g_ref[...], b_ref[...])
    gate = 1.0 / (1.0 + jnp.exp(-(_dot(y.astype(BF16), wg_ref[...]) + bg_ref[...])))
    o_ref[...] = y + gate * _dot(p_ref[...].astype(BF16), wp_ref[...])


def ln_ple(x2d, f2d, p2d, g, b, wg, bg, wp, alpha, tm=512):
    T = x2d.shape[0]
    tm = math.gcd(T, tm)
    row = lambda i: (i, 0)
    return pl.pallas_call(
        functools.partial(_ln_ple_kernel, alpha=alpha),
        grid=(T // tm,),
        in_specs=[pl.BlockSpec((tm, D_MODEL), row), pl.BlockSpec((tm, D_MODEL), row),
                  pl.BlockSpec((tm, PLE_DIM), row),
                  _full(g.shape), _full(b.shape), _full(wg.shape), _full(bg.shape), _full(wp.shape)],
        out_specs=pl.BlockSpec((tm, D_MODEL), row),
        out_shape=jax.ShapeDtypeStruct((T, D_MODEL), F32),
        compiler_params=_params("parallel"),
        name="ln_ple",
    )(x2d, f2d, p2d, g, b, wg, bg, wp)


def _rot_cols(w):
    half = MLA_ROPE // 2
    return jnp.concatenate([-w[..., half:], w[..., :half]], axis=-1)


def _mla_weights(w_a, q_norm, kv_norm, w_uq, w_ukv):
    zk = jnp.zeros((D_MODEL, LANES - MLA_ROPE), F32)
    w_kr = w_a[:, MLA_Q_LORA + MLA_KV_LORA:]
    wa = jnp.concatenate([w_a[:, :MLA_Q_LORA + MLA_KV_LORA], w_kr, zk, _rot_cols(w_kr), zk], axis=1)
    uq = w_uq.reshape(MLA_Q_LORA, MLA_HEADS, MLA_NOPE + MLA_ROPE)
    zq = jnp.zeros((MLA_Q_LORA, MLA_HEADS, MLA_QK_PAD - MLA_NOPE - MLA_ROPE), F32)
    wuq = jnp.concatenate([uq, zq], axis=-1).reshape(MLA_Q_LORA, MLA_HEADS * MLA_QK_PAD)
    wuqr = jnp.concatenate([jnp.zeros_like(uq[..., :MLA_NOPE]), _rot_cols(uq[..., MLA_NOPE:]), zq],
                           axis=-1).reshape(MLA_Q_LORA, MLA_HEADS * MLA_QK_PAD)
    ukv = w_ukv.reshape(MLA_KV_LORA, MLA_HEADS, MLA_NOPE + MLA_V)
    return {
        "wa": wa.astype(BF16),
        "qn": q_norm.astype(F32).reshape(1, -1),
        "kvn": kv_norm.astype(F32).reshape(1, -1),
        "wuq": wuq.astype(BF16),
        "wuqr": wuqr.astype(BF16),
        "wuk": ukv[..., :MLA_NOPE].reshape(MLA_KV_LORA, -1).astype(BF16),
        "wuv": ukv[..., MLA_NOPE:].reshape(MLA_KV_LORA, -1).astype(BF16),
    }


def _rope_tabs(seq):
    inv_freq = 1.0 / (ROPE_THETA ** (jnp.arange(0, MLA_ROPE, 2, dtype=F32) / MLA_ROPE))
    ang = jnp.arange(seq, dtype=F32)[:, None] * inv_freq[None, :]
    emb = jnp.concatenate([ang, ang], axis=-1)
    cos, sin = jnp.cos(emb), jnp.sin(emb)
    one = jnp.ones((seq, MLA_NOPE), F32)
    z64 = jnp.zeros((seq, LANES - MLA_ROPE), F32)
    z128 = jnp.zeros((seq, MLA_NOPE), F32)
    return (jnp.concatenate([one, cos, z64], axis=1), jnp.concatenate([z128, sin, z64], axis=1),
            jnp.concatenate([cos, z64], axis=1), jnp.concatenate([sin, z64], axis=1))


def _trunk(x, p, lbs, mla_w, hgrn_w, peer_w_, ln_w, ple_w, depth):
    batch, seq, _ = x.shape
    T = batch * seq
    alpha = (2 * depth) ** 0.25
    x2d = x.reshape(T, D_MODEL)
    tabs = _rope_tabs(seq)
    for i in range(depth):
        j = i // 2
        g1, b1, g2, b2 = ln_w[i]
        if i % 2 == 0:
            w = mla_w[j]
            q2d, k2d, v2d = mla_proj(x2d, seq, w, tabs)
            o2d = attention(q2d, k2d.T, v2d, batch, seq)
            x1, x1b = proj_res_ln(o2d, x2d, w["wo"], g1, b1, alpha)
        else:
            w = hgrn_w[j]
            proj = matmul(x2d, w["win"])
            o2 = hgrn_scan(proj, lbs[:, j], batch, seq)
            x1, x1b = hgrn_out(o2, proj, x2d, w["nw"], w["wo"], g1, b1, alpha)
        pw = peer_w_[i]
        gate, eid = peer_topk(x1b, pw["wq"], pw["keys"])
        wmat = peer_w(gate, eid)
        f2d = peer_dense(x1b, pw["ut"], pw["v"], wmat)
        wg, bg, wp = ple_w[i]
        x2d = ln_ple(x1, f2d, p[i].reshape(T, PLE_DIM), g2, b2, wg, bg, wp, alpha)
    return x2d.reshape(batch, seq, D_MODEL)


def kernel(x_prompt, x_sample, p_prompt, p_sample, mla_w_a, mla_q_norm, mla_kv_norm, mla_w_uq, mla_w_ukv, mla_w_o, hgrn_w_in, hgrn_lb, hgrn_norm, hgrn_w_o, peer_w_q, peer_sub_keys, peer_u, peer_v, ln1_g, ln1_b, ln2_g, ln2_b, ple_gate_w, ple_gate_b, ple_proj):
    depth = peer_w_q.shape[0]
    lbs = lower_bounds(hgrn_lb)
    mla_w = []
    for j in range(mla_w_a.shape[0]):
        w = _mla_weights(mla_w_a[j], mla_q_norm[j], mla_kv_norm[j], mla_w_uq[j], mla_w_ukv[j])
        w["wo"] = mla_w_o[j].astype(BF16)
        mla_w.append(w)
    hgrn_w = [{"win": hgrn_w_in[j].astype(BF16), "nw": hgrn_norm[j].astype(F32).reshape(1, -1),
               "wo": hgrn_w_o[j].astype(BF16)} for j in range(hgrn_w_in.shape[0])]
    peer_w_ = [{"wq": peer_w_q[i].astype(BF16), "keys": peer_sub_keys[i].astype(BF16),
                "ut": peer_u[i].astype(BF16).T, "v": peer_v[i].astype(BF16)} for i in range(depth)]
    r = lambda a: a.astype(F32).reshape(1, -1)
    ln_w = [(r(ln1_g[i]), r(ln1_b[i]), r(ln2_g[i]), r(ln2_b[i])) for i in range(depth)]
    ple_w = [(ple_gate_w[i].astype(BF16), r(ple_gate_b[i]), ple_proj[i].astype(BF16)) for i in range(depth)]
    args = (lbs, mla_w, hgrn_w, peer_w_, ln_w, ple_w, depth)
    return (_trunk(x_prompt, p_prompt, *args), _trunk(x_sample, p_sample, *args))
```

```python
import functools
import math

import numpy as np
import jax
import jax.numpy as jnp
from jax import lax
from jax.experimental import pallas as pl
from jax.experimental.pallas import tpu as pltpu

D_MODEL = 1024
PLE_DIM = 256
MLA_HEADS = 8
MLA_NOPE = 128
MLA_ROPE = 64
MLA_V = 128
MLA_Q_LORA = 384
MLA_KV_LORA = 256
MLA_SCALE = (MLA_NOPE + MLA_ROPE) ** -0.5
MLA_QK_PAD = 256
LOG2E = math.log2(math.e)
ROPE_THETA = 10000.0
HGRN_HEADS = 8
HGRN_DK = 128
HGRN_DV = 128
GATE_FLOOR = 1e-30
PEER_HEADS = 8
PEER_DK = 256
N_KEYS = 128
N_EXPERTS = N_KEYS * N_KEYS
PEER_TOPK = 16
LN_EPS = 1e-5
RMS_EPS = 1e-6

LANES = 128
VMEM_LIMIT = 56 * 1024 * 1024

SCAN_CHUNK = 128
SCAN_LEVELS = 7
W_PAD = 4
W_PITCH = N_KEYS + W_PAD
W_LHS_PAD = 16
W_GROUP = 32
W_ROWS = 8

F32 = jnp.float32
BF16 = jnp.bfloat16

_NT = (((1,), (1,)), ((), ()))


def _params(*sem):
    return pltpu.CompilerParams(dimension_semantics=sem, vmem_limit_bytes=VMEM_LIMIT)


def _dot(a, b):
    return jnp.dot(a, b, preferred_element_type=F32)


def _dot_nt(a, b):
    return lax.dot_general(a, b, _NT, preferred_element_type=F32)


def _layer_norm(y, g, b):
    mu = jnp.mean(y, axis=-1, keepdims=True)
    yc = y - mu
    var = jnp.mean(yc * yc, axis=-1, keepdims=True)
    return yc * lax.rsqrt(var + LN_EPS) * g + b


def _silu(t):
    return t * (1.0 / (1.0 + jnp.exp(-t)))


def _full(shape):
    nd = len(shape)
    return pl.BlockSpec(shape, lambda *_: (0,) * nd)


def _lower_bounds_kernel(lb_ref, out_ref):
    n = lb_ref.shape[1]
    for d in range(lb_ref.shape[0]):
        rows = [lb_ref[d, j] for j in range(n)]
        m = rows[0]
        for r in rows[1:]:
            m = jnp.maximum(m, r)
        ex = [jnp.exp(r - m) for r in rows]
        tot = ex[0]
        for e in ex[1:]:
            tot = tot + e
        sm = [e / tot for e in ex]
        cum = sm[0]
        out_ref[d, 0] = cum - sm[0]
        for j in range(1, n):
            cum = cum + sm[j]
            out_ref[d, j] = cum - sm[0]


def lower_bounds(hgrn_lb):
    two, n, hk = hgrn_lb.shape
    lb4 = hgrn_lb.astype(F32).reshape(two, n, 1, hk)
    out = pl.pallas_call(
        _lower_bounds_kernel,
        out_shape=jax.ShapeDtypeStruct((two, n, 1, hk), F32),
        name="hgrn_lower_bounds",
    )(lb4)
    return out


def _mla_proj_kernel(x_ref, wa_ref, qn_ref, kvn_ref, wuq_ref, wuqr_ref, wuk_ref, wuv_ref,
                     cq_ref, sq_ref, ck_ref, sk_ref, q_ref, k_ref, v_ref):
    xb = x_ref[...].astype(BF16)
    a = _dot(xb, wa_ref[...])
    c_q = a[:, :MLA_Q_LORA]
    c_kv = a[:, MLA_Q_LORA:MLA_Q_LORA + MLA_KV_LORA]
    kr = a[:, 640:768]
    krr = a[:, 768:896]
    cqn = (c_q * lax.rsqrt(jnp.mean(c_q * c_q, axis=-1, keepdims=True) + RMS_EPS) * qn_ref[...]).astype(BF16)
    ckvn = (c_kv * lax.rsqrt(jnp.mean(c_kv * c_kv, axis=-1, keepdims=True) + RMS_EPS) * kvn_ref[...]).astype(BF16)
    q = _dot(cqn, wuq_ref[...])
    qr = _dot(cqn, wuqr_ref[...])
    cq = cq_ref[...]
    sq = sq_ref[...]
    kro = (kr * ck_ref[...] + krr * sk_ref[...]).astype(BF16)
    kn = _dot(ckvn, wuk_ref[...]).astype(BF16)
    v_ref[...] = _dot(ckvn, wuv_ref[...]).astype(BF16)
    for h in range(MLA_HEADS):
        lo = h * MLA_QK_PAD
        qh = (q[:, lo:lo + MLA_QK_PAD] * cq + qr[:, lo:lo + MLA_QK_PAD] * sq) * (MLA_SCALE * LOG2E)
        q_ref[:, lo:lo + MLA_QK_PAD] = qh.astype(BF16)
        k_ref[:, lo:lo + MLA_NOPE] = kn[:, h * MLA_NOPE:(h + 1) * MLA_NOPE]
        k_ref[:, lo + MLA_NOPE:lo + MLA_QK_PAD] = kro


def mla_proj(x2d, seq, w, tabs, tm=256):
    T = x2d.shape[0]
    tm = math.gcd(seq, tm)
    nper = seq // tm
    cq, sq, ck, sk = tabs
    row = lambda i: (i, 0)
    pos = lambda i: (i % nper, 0)
    hq = MLA_HEADS * MLA_QK_PAD
    return pl.pallas_call(
        _mla_proj_kernel,
        grid=(T // tm,),
        in_specs=[
            pl.BlockSpec((tm, D_MODEL), row),
            _full(w["wa"].shape), _full(w["qn"].shape), _full(w["kvn"].shape),
            _full(w["wuq"].shape), _full(w["wuqr"].shape), _full(w["wuk"].shape), _full(w["wuv"].shape),
            pl.BlockSpec((tm, MLA_QK_PAD), pos), pl.BlockSpec((tm, MLA_QK_PAD), pos),
            pl.BlockSpec((tm, LANES), pos), pl.BlockSpec((tm, LANES), pos),
        ],
        out_specs=[pl.BlockSpec((tm, hq), row), pl.BlockSpec((tm, hq), row),
                   pl.BlockSpec((tm, MLA_HEADS * MLA_V), row)],
        out_shape=[jax.ShapeDtypeStruct((T, hq), BF16), jax.ShapeDtypeStruct((T, hq), BF16),
                   jax.ShapeDtypeStruct((T, MLA_HEADS * MLA_V), BF16)],
        compiler_params=_params("parallel"),
        name="mla_proj",
    )(x2d, w["wa"], w["qn"], w["kvn"], w["wuq"], w["wuqr"], w["wuk"], w["wuv"], cq, sq, ck, sk)


def _attn_kernel(q_ref, kt_ref, v_ref, o_ref, *, tk, unroll):
    q = q_ref[...]
    tq = q.shape[0]
    nk = kt_ref.shape[1] // tk

    def body(c, carry):
        m, l, acc = carry
        for u in range(unroll):
            off = pl.multiple_of((c * unroll + u) * tk, tk)
            s = _dot(q, kt_ref[:, pl.ds(off, tk)])
            m_new = jnp.maximum(m, jnp.max(s, axis=-1, keepdims=True))
            corr = jnp.exp2(m - m_new)
            p = jnp.exp2(s - m_new)
            l = corr * l + jnp.sum(p, axis=-1, keepdims=True)
            acc = corr * acc + _dot(p.astype(BF16), v_ref[pl.ds(off, tk), :])
            m = m_new
        return m, l, acc

    init = (jnp.full((tq, 1), -jnp.inf, F32), jnp.zeros((tq, 1), F32), jnp.zeros((tq, MLA_V), F32))
    _, l, acc = lax.fori_loop(0, nk // unroll, body, init)
    o_ref[...] = (acc / l).astype(o_ref.dtype)


def attention(q2d, kt2d, v2d, batch, seq, tq=512, tk=512, unroll=4):
    T = q2d.shape[0]
    tq, tk = math.gcd(seq, tq), math.gcd(seq, tk)
    unroll = math.gcd(seq // tk, unroll)
    nq = seq // tq
    return pl.pallas_call(
        functools.partial(_attn_kernel, tk=tk, unroll=unroll),
        grid=(batch, MLA_HEADS, nq),
        in_specs=[
            pl.BlockSpec((tq, MLA_QK_PAD), lambda b, h, i: (b * nq + i, h)),
            pl.BlockSpec((MLA_QK_PAD, seq), lambda b, h, i: (h, b)),
            pl.BlockSpec((seq, MLA_V), lambda b, h, i: (b, h)),
        ],
        out_specs=pl.BlockSpec((tq, MLA_V), lambda b, h, i: (b * nq + i, h)),
        out_shape=jax.ShapeDtypeStruct((T, MLA_HEADS * MLA_V), BF16),
        compiler_params=_params("parallel", "parallel", "arbitrary"),
        name="mla_attention",
    )(q2d, kt2d, v2d)


def _proj_res_ln_kernel(a_ref, x_ref, w_ref, g_ref, b_ref, o_ref, ob_ref, *, alpha):
    h = _dot(a_ref[...], w_ref[...])
    y = _layer_norm(alpha * x_ref[...] + h, g_ref[...], b_ref[...])
    o_ref[...] = y
    ob_ref[...] = y.astype(BF16)


def proj_res_ln(a2d, x2d, w, g, b, alpha, tm=512):
    T = x2d.shape[0]
    tm = math.gcd(T, tm)
    row = lambda i: (i, 0)
    return pl.pallas_call(
        functools.partial(_proj_res_ln_kernel, alpha=alpha),
        grid=(T // tm,),
        in_specs=[pl.BlockSpec((tm, a2d.shape[1]), row), pl.BlockSpec((tm, D_MODEL), row),
                  _full(w.shape), _full(g.shape), _full(b.shape)],
        out_specs=[pl.BlockSpec((tm, D_MODEL), row), pl.BlockSpec((tm, D_MODEL), row)],
        out_shape=[jax.ShapeDtypeStruct((T, D_MODEL), F32), jax.ShapeDtypeStruct((T, D_MODEL), BF16)],
        compiler_params=_params("parallel"),
        name="proj_res_ln",
    )(a2d, x2d, w, g, b)


def _matmul_kernel(x_ref, w_ref, o_ref):
    o_ref[...] = _dot(x_ref[...].astype(BF16), w_ref[...])


def matmul(x2d, w, tm=512, tn=1280):
    T, K = x2d.shape
    N = w.shape[1]
    tm = math.gcd(T, tm)
    return pl.pallas_call(
        _matmul_kernel,
        grid=(T // tm, N // tn),
        in_specs=[pl.BlockSpec((tm, K), lambda i, j: (i, 0)), pl.BlockSpec((K, tn), lambda i, j: (0, j))],
        out_specs=pl.BlockSpec((tm, tn), lambda i, j: (i, j)),
        out_shape=jax.ShapeDtypeStruct((T, N), F32),
        compiler_params=_params("parallel", "arbitrary"),
        name="matmul",
    )(x2d, w)


def _scan_tables():
    C, L = SCAN_CHUNK, SCAN_LEVELS
    mats = np.zeros((2, (2 + L) * C, C), np.float32)
    lvl = np.full((2, C, C), -1, np.int32)
    for d in range(2):
        p = np.arange(C) if d == 0 else C - 1 - np.arange(C)
        pt, pu = p[:, None], p[None, :]
        mats[d, 0:C] = pu <= pt
        mats[d, C:2 * C] = pu > pt
        for l in range(L):
            m = 1 << l
            r = (pt // (2 * m)) * (2 * m) + m - 1
            qside = (pt % (2 * m)) >= m
            e = np.where(qside, (pu > r) & (pu <= pt), (pu > pt) & (pu <= r))
            mats[d, (2 + l) * C:(3 + l) * C] = e
            same = (pt // (2 * m)) == (pu // (2 * m))
            lvl[d][same & qside & ((pu % (2 * m)) < m)] = l
        lvl[d][pt == pu] = L
    return jnp.asarray(np.concatenate([mats, mats], axis=2), BF16), jnp.asarray(np.concatenate([lvl, lvl], axis=2))


def _hgrn_scan_kernel(q_ref, z_ref, v_ref, lb_ref, mat_ref, lvl_ref, o_ref, state_ref, kk_ref, ex_ref):
    C, L = SCAN_CHUNK, SCAN_LEVELS

    @pl.when(pl.program_id(2) == 0)
    def _():
        state_ref[...] = jnp.zeros_like(state_ref)

    z = z_ref[...]
    lb = lb_ref[...]
    e = jnp.exp(-jnp.abs(z))
    r = 1.0 / (1.0 + e)
    er = e * r
    pos = z >= 0
    f = lb + (1.0 - lb) * jnp.where(pos, r, er)
    g = jnp.log(jnp.maximum(f, GATE_FLOOR))
    kk_ref[...] = (1.0 - lb) * jnp.where(pos, er, r)
    g1 = g.astype(BF16)
    g2 = (g - g1.astype(F32)).astype(BF16)
    ex_ref[...] = _dot(mat_ref[...], jnp.concatenate([g1, g2], axis=0))

    lvl = lvl_ref[...]
    zero = jnp.zeros((C, HGRN_DK), BF16)
    wide = 2 * HGRN_DK

    def block_diag(x):
        return jnp.concatenate([jnp.concatenate([x[:, :HGRN_DK], zero], axis=1),
                                jnp.concatenate([zero, x[:, HGRN_DK:]], axis=1)], axis=0)

    for hp in range(HGRN_HEADS // 2):
        sl = slice(hp * wide, (hp + 1) * wide)
        q = _silu(q_ref[:, sl])
        kk = kk_ref[:, sl]
        qb = q.astype(BF16)
        kb = kk.astype(BF16)
        attn = jnp.where(lvl == L, _dot_nt(qb, block_diag(kb)), 0.0)
        for l in range(L):
            xl = jnp.exp(ex_ref[(2 + l) * C:(3 + l) * C, sl]).astype(BF16)
            p = _dot_nt(qb * xl, block_diag(kb * xl))
            attn = jnp.where(lvl == l, p, attn)
        v2 = v_ref[:, sl]
        e_b = ex_ref[0:C, sl]
        e_last = ex_ref[C:2 * C, sl]
        b_last = e_b[0:1] + e_last[0:1]
        st = [state_ref[2 * hp + d] for d in range(2)]
        st_bd = jnp.concatenate([jnp.concatenate([st[0].astype(BF16), zero], axis=1),
                                 jnp.concatenate([zero, st[1].astype(BF16)], axis=1)], axis=0)
        o = _dot(attn.astype(BF16), block_diag(v2.astype(BF16)))
        o_ref[:, sl] = o + _dot_nt((q * jnp.exp(e_b)).astype(BF16), st_bd)
        kd = (kk * jnp.exp(e_last)).astype(BF16)
        for d in range(2):
            hs = slice(d * HGRN_DK, (d + 1) * HGRN_DK)
            vt = v2[:, hs].T.astype(BF16)
            state_ref[2 * hp + d] = st[d] * jnp.exp(b_last[:, hs]) + _dot(vt, kd[:, hs])


def hgrn_scan(proj, lb, batch, seq):
    T = proj.shape[0]
    C = SCAN_CHUNK
    nc = seq // C
    hk = HGRN_HEADS * HGRN_DK
    mats, lvl = _scan_tables()

    def rowblk(b, d, c):
        return b * nc + c + d * (nc - 1 - 2 * c)

    return pl.pallas_call(
        _hgrn_scan_kernel,
        grid=(batch, 2, nc),
        in_specs=[
            pl.BlockSpec((C, hk), lambda b, d, c: (rowblk(b, d, c), 0)),
            pl.BlockSpec((C, hk), lambda b, d, c: (rowblk(b, d, c), 1 + d)),
            pl.BlockSpec((C, hk), lambda b, d, c: (rowblk(b, d, c), 3)),
            pl.BlockSpec((None, 1, hk), lambda b, d, c: (d, 0, 0)),
            pl.BlockSpec((None,) + mats.shape[1:], lambda b, d, c: (d, 0, 0)),
            pl.BlockSpec((None, C, 2 * C), lambda b, d, c: (d, 0, 0)),
        ],
        out_specs=pl.BlockSpec((None, C, hk), lambda b, d, c: (d, rowblk(b, d, c), 0)),
        out_shape=jax.ShapeDtypeStruct((2, T, hk), F32),
        scratch_shapes=[pltpu.VMEM((HGRN_HEADS, HGRN_DV, HGRN_DK), F32), pltpu.VMEM((C, hk), F32),
                        pltpu.VMEM(((2 + SCAN_LEVELS) * C, hk), F32)],
        compiler_params=_params("parallel", "arbitrary", "arbitrary"),
        name="hgrn_scan",
    )(proj, proj, proj, lb, mats, lvl)


def _hgrn_out_kernel(o_ref, g_ref, x_ref, nw_ref, w_ref, lg_ref, lbias_ref, y_ref, yb_ref, a_ref, *, alpha):
    nw = nw_ref[...]
    for h in range(HGRN_HEADS):
        sl = slice(h * HGRN_DV, (h + 1) * HGRN_DV)
        o = o_ref[0, :, sl] + o_ref[1, :, sl]
        on = o * lax.rsqrt(jnp.mean(o * o, axis=-1, keepdims=True) + RMS_EPS) * nw
        a_ref[:, sl] = (on * _silu(g_ref[:, sl])).astype(BF16)
    hmix = _dot(a_ref[...], w_ref[...])
    y = _layer_norm(alpha * x_ref[...] + hmix, lg_ref[...], lbias_ref[...])
    y_ref[...] = y
    yb_ref[...] = y.astype(BF16)


def hgrn_out(o2, proj, x2d, nw, w, g, b, alpha, tm=256):
    T = x2d.shape[0]
    tm = math.gcd(T, tm)
    hv = HGRN_HEADS * HGRN_DV
    row = lambda i: (i, 0)
    return pl.pallas_call(
        functools.partial(_hgrn_out_kernel, alpha=alpha),
        grid=(T // tm,),
        in_specs=[pl.BlockSpec((2, tm, hv), lambda i: (0, i, 0)),
                  pl.BlockSpec((tm, hv), lambda i: (i, 4)),
                  pl.BlockSpec((tm, D_MODEL), row),
                  _full(nw.shape), _full(w.shape), _full(g.shape), _full(b.shape)],
        out_specs=[pl.BlockSpec((tm, D_MODEL), row), pl.BlockSpec((tm, D_MODEL), row)],
        out_shape=[jax.ShapeDtypeStruct((T, D_MODEL), F32), jax.ShapeDtypeStruct((T, D_MODEL), BF16)],
        scratch_shapes=[pltpu.VMEM((tm, hv), BF16)],
        compiler_params=_params("parallel"),
        name="hgrn_out",
    )(o2, proj, x2d, nw, w, g, b)


_PAIRS = [(a, b) for a in range(PEER_TOPK) for b in range(PEER_TOPK) if (a + 1) * (b + 1) <= PEER_TOPK]
_NPAIR_PAD = -(-len(_PAIRS) // 8) * 8


def _top_rows(s, k, payload=None):
    n = s.shape[0]
    iota = lax.broadcasted_iota(jnp.int32, s.shape, 0)
    sub = 8
    iota8 = lax.broadcasted_iota(jnp.int32, (sub, s.shape[1]), 0)
    vals, idxs = [], []
    for _ in range(k):
        nodes = [(s[r:r + sub], iota8 + r) for r in range(0, n, sub)]
        while len(nodes) > 1:
            nxt = [(jnp.maximum(va, vb), jnp.where(va >= vb, ia, ib))
                   for (va, ia), (vb, ib) in zip(nodes[0::2], nodes[1::2])]
            nodes = nxt + ([nodes[-1]] if len(nodes) % 2 else [])
        v8, i8 = nodes[0]
        m = jnp.max(v8, axis=0, keepdims=True)
        im = jnp.min(jnp.where(v8 == m, i8, n), axis=0, keepdims=True)
        hit = iota == im
        vals.append(m)
        if payload is None:
            idxs.append(im)
        else:
            idxs.append(jnp.max(jnp.where(hit, payload, -1), axis=0, keepdims=True))
        s = jnp.where(hit, -jnp.inf, s)
    return vals, idxs


def _peer_topk_kernel(xb_ref, wq_ref, keys_ref, gate_ref, eid_ref, q_ref):
    tm = xb_ref.shape[0]
    q_ref[...] = _dot(xb_ref[...], wq_ref[...]).astype(BF16)
    half = PEER_DK // 2
    neg = jnp.full((_NPAIR_PAD - len(_PAIRS), tm), -jnp.inf, F32)
    zero = jnp.zeros((_NPAIR_PAD - len(_PAIRS), tm), jnp.int32)

    def head(h, carry):
        sv, si = [], []
        for c in range(2):
            off = pl.multiple_of(h * PEER_DK + c * half, half)
            s = _dot_nt(keys_ref[c], q_ref[:, pl.ds(off, half)])
            v_, i_ = _top_rows(s, PEER_TOPK)
            sv.append(v_)
            si.append(i_)
        cand = jnp.concatenate([sv[0][a] + sv[1][b] for a, b in _PAIRS] + [neg], axis=0)
        cid = jnp.concatenate([si[0][a] * N_KEYS + si[1][b] for a, b in _PAIRS] + [zero], axis=0)
        tv, te = _top_rows(cand, PEER_TOPK, payload=cid)
        tv = jnp.concatenate(tv, axis=0)
        ex = jnp.exp(tv - tv[0:1])
        gate = ex / jnp.sum(ex, axis=0, keepdims=True)
        row = pl.multiple_of(h * PEER_TOPK, PEER_TOPK)
        gate_ref[pl.ds(row, PEER_TOPK), :] = gate
        eid_ref[pl.ds(row, PEER_TOPK), :] = jnp.concatenate(te, axis=0)
        return carry

    lax.fori_loop(0, PEER_HEADS, head, 0)


def peer_topk(xb, wq, keys, tm=256):
    T = xb.shape[0]
    tm = math.gcd(T, tm)
    ns = PEER_HEADS * PEER_TOPK
    return pl.pallas_call(
        _peer_topk_kernel,
        grid=(T // tm,),
        in_specs=[pl.BlockSpec((tm, D_MODEL), lambda i: (i, 0)), _full(wq.shape), _full(keys.shape)],
        out_specs=[pl.BlockSpec((ns, tm), lambda i: (0, i)), pl.BlockSpec((ns, tm), lambda i: (0, i))],
        out_shape=[jax.ShapeDtypeStruct((ns, T), F32), jax.ShapeDtypeStruct((ns, T), jnp.int32)],
        scratch_shapes=[pltpu.VMEM((tm, PEER_HEADS * PEER_DK), BF16)],
        compiler_params=_params("parallel"),
        name="peer_topk",
    )(xb, wq, keys)


def _peer_w_kernel(gate_ref, eid_ref, w_ref, gt_ref, it_ref, jt_ref, tile_ref):
    tm = gate_ref.shape[1]
    eid_t = eid_ref[...].T
    gt_ref[...] = gate_ref[...].T
    it_ref[...] = lax.shift_right_logical(eid_t, int(math.log2(N_KEYS)))
    jt_ref[...] = lax.bitwise_and(eid_t, N_KEYS - 1)
    ns = gate_ref.shape[0]
    iota = lax.broadcasted_iota(jnp.int32, (N_KEYS, ns), 0)
    iota_pad = lax.broadcasted_iota(jnp.int32, (N_KEYS + W_LHS_PAD, ns), 0)
    zero = jnp.zeros((N_KEYS, ns), BF16)

    def token_group(p, carry):
        base = pl.multiple_of(p * W_GROUP, W_GROUP)
        g8 = gt_ref[pl.ds(base, W_GROUP), :]
        i8 = it_ref[pl.ds(base, W_GROUP), :]
        j8 = jt_ref[pl.ds(base, W_GROUP), :]
        for u in range(0, W_GROUP, 2):
            at0 = jnp.where(iota_pad == i8[u:u + 1], g8[u:u + 1], 0.0).astype(BF16)
            at1 = jnp.where(iota_pad == i8[u + 1:u + 2] + W_PAD, g8[u + 1:u + 2], 0.0).astype(BF16)
            bt = [jnp.where(iota == j8[u + d:u + d + 1], 1.0, 0.0).astype(BF16) for d in range(2)]
            lhs = jnp.concatenate([at0, at1], axis=1)
            rhs = jnp.concatenate([jnp.concatenate([bt[0], zero], axis=1),
                                   jnp.concatenate([zero, bt[1]], axis=1)], axis=0)
            res = _dot_nt(lhs, rhs)
            row = pl.multiple_of((base + u) * W_PITCH, 8)
            tile_ref[pl.ds(row, N_KEYS), :] = res[:N_KEYS, :N_KEYS]
            tile_ref[pl.ds(row + N_KEYS, N_KEYS + 2 * W_PAD), :] = res[:N_KEYS + 2 * W_PAD, N_KEYS:]
        return carry

    lax.fori_loop(0, tm // W_GROUP, token_group, 0)

    def key_rows(i2, carry):
        for d in range(W_ROWS):
            i = i2 * W_ROWS + d
            col = tile_ref[pl.ds(i, tm, stride=W_PITCH), :]
            w_ref[:, pl.ds(pl.multiple_of(i * N_KEYS, N_KEYS), N_KEYS)] = col.astype(BF16)
        return carry

    lax.fori_loop(0, N_KEYS // W_ROWS, key_rows, 0)


def peer_w(gate, eid, tm=128):
    ns, T = gate.shape
    tm = math.gcd(T, tm)
    return pl.pallas_call(
        _peer_w_kernel,
        grid=(T // tm,),
        in_specs=[pl.BlockSpec((ns, tm), lambda i: (0, i)), pl.BlockSpec((ns, tm), lambda i: (0, i))],
        out_specs=pl.BlockSpec((tm, N_EXPERTS), lambda i: (i, 0)),
        out_shape=jax.ShapeDtypeStruct((T, N_EXPERTS), BF16),
        scratch_shapes=[pltpu.VMEM((tm, ns), F32), pltpu.VMEM((tm, ns), jnp.int32),
                        pltpu.VMEM((tm, ns), jnp.int32), pltpu.VMEM((tm * W_PITCH, N_KEYS), F32)],
        compiler_params=_params("parallel"),
        name="peer_gate_matrix",
    )(gate, eid)


def _peer_dense_kernel(xb_ref, ut_ref, v_ref, w_ref, o_ref):
    @pl.when(pl.program_id(1) == 0)
    def _():
        o_ref[...] = jnp.zeros_like(o_ref)

    h = _dot(xb_ref[...], ut_ref[...])
    act = 0.5 * h * (1.0 + lax.erf(h * (1.0 / math.sqrt(2.0))))
    hw = (act * w_ref[...].astype(F32)).astype(BF16)
    o_ref[...] += _dot(hw, v_ref[...])


def peer_dense(xb, ut, v, wmat, tm=512, ne=2048):
    T = xb.shape[0]
    tm = math.gcd(T, tm)
    return pl.pallas_call(
        _peer_dense_kernel,
        grid=(T // tm, N_EXPERTS // ne),
        in_specs=[pl.BlockSpec((tm, D_MODEL), lambda i, j: (i, 0)),
                  pl.BlockSpec((D_MODEL, ne), lambda i, j: (0, j)),
                  pl.BlockSpec((ne, D_MODEL), lambda i, j: (j, 0)),
                  pl.BlockSpec((tm, ne), lambda i, j: (i, j))],
        out_specs=pl.BlockSpec((tm, D_MODEL), lambda i, j: (i, 0)),
        out_shape=jax.ShapeDtypeStruct((T, D_MODEL), F32),
        compiler_params=_params("parallel", "arbitrary"),
        name="peer_dense",
    )(xb, ut, v, wmat)


def _ln_ple_kernel(x_ref, f_ref, p_ref, g_ref, b_ref, wg_ref, bg_ref, wp_ref, o_ref, *, alpha):
    y = _layer_norm(alpha * x_ref[...] + f_ref[...], g_ref[...], b_ref[...])
    gate = 1.0 / (1.0 + jnp.exp(-(_dot(y.astype(BF16), wg_ref[...]) + bg_ref[...])))
    o_ref[...] = y + gate * _dot(p_ref[...].astype(BF16), wp_ref[...])


def ln_ple(x2d, f2d, p2d, g, b, wg, bg, wp, alpha, tm=512):
    T = x2d.shape[0]
    tm = math.gcd(T, tm)
    row = lambda i: (i, 0)
    return pl.pallas_call(
        functools.partial(_ln_ple_kernel, alpha=alpha),
        grid=(T // tm,),
        in_specs=[pl.BlockSpec((tm, D_MODEL), row), pl.BlockSpec((tm, D_MODEL), row),
                  pl.BlockSpec((tm, PLE_DIM), row),
                  _full(g.shape), _full(b.shape), _full(wg.shape), _full(bg.shape), _full(wp.shape)],
        out_specs=pl.BlockSpec((tm, D_MODEL), row),
        out_shape=jax.ShapeDtypeStruct((T, D_MODEL), F32),
        compiler_params=_params("parallel"),
        name="ln_ple",
    )(x2d, f2d, p2d, g, b, wg, bg, wp)


def _rot_cols(w):
    half = MLA_ROPE // 2
    return jnp.concatenate([-w[..., half:], w[..., :half]], axis=-1)


def _mla_weights(w_a, q_norm, kv_norm, w_uq, w_ukv):
    zk = jnp.zeros((D_MODEL, LANES - MLA_ROPE), F32)
    w_kr = w_a[:, MLA_Q_LORA + MLA_KV_LORA:]
    wa = jnp.concatenate([w_a[:, :MLA_Q_LORA + MLA_KV_LORA], w_kr, zk, _rot_cols(w_kr), zk], axis=1)
    uq = w_uq.reshape(MLA_Q_LORA, MLA_HEADS, MLA_NOPE + MLA_ROPE)
    zq = jnp.zeros((MLA_Q_LORA, MLA_HEADS, MLA_QK_PAD - MLA_NOPE - MLA_ROPE), F32)
    wuq = jnp.concatenate([uq, zq], axis=-1).reshape(MLA_Q_LORA, MLA_HEADS * MLA_QK_PAD)
    wuqr = jnp.concatenate([jnp.zeros_like(uq[..., :MLA_NOPE]), _rot_cols(uq[..., MLA_NOPE:]), zq],
                           axis=-1).reshape(MLA_Q_LORA, MLA_HEADS * MLA_QK_PAD)
    ukv = w_ukv.reshape(MLA_KV_LORA, MLA_HEADS, MLA_NOPE + MLA_V)
    return {
        "wa": wa.astype(BF16),
        "qn": q_norm.astype(F32).reshape(1, -1),
        "kvn": kv_norm.astype(F32).reshape(1, -1),
        "wuq": wuq.astype(BF16),
        "wuqr": wuqr.astype(BF16),
        "wuk": ukv[..., :MLA_NOPE].reshape(MLA_KV_LORA, -1).astype(BF16),
        "wuv": ukv[..., MLA_NOPE:].reshape(MLA_KV_LORA, -1).astype(BF16),
    }


def _rope_tabs(seq):
    inv_freq = 1.0 / (ROPE_THETA ** (jnp.arange(0, MLA_ROPE, 2, dtype=F32) / MLA_ROPE))
    ang = jnp.arange(seq, dtype=F32)[:, None] * inv_freq[None, :]
    emb = jnp.concatenate([ang, ang], axis=-1)
    cos, sin = jnp.cos(emb), jnp.sin(emb)
    one = jnp.ones((seq, MLA_NOPE), F32)
    z64 = jnp.zeros((seq, LANES - MLA_ROPE), F32)
    z128 = jnp.zeros((seq, MLA_NOPE), F32)
    return (jnp.concatenate([one, cos, z64], axis=1), jnp.concatenate([z128, sin, z64], axis=1),
            jnp.concatenate([cos, z64], axis=1), jnp.concatenate([sin, z64], axis=1))


def _trunk(x, p, lbs, mla_w, hgrn_w, peer_w_, ln_w, ple_w, depth):
    batch, seq, _ = x.shape
    T = batch * seq
    alpha = (2 * depth) ** 0.25
    x2d = x.reshape(T, D_MODEL)
    tabs = _rope_tabs(seq)
    for i in range(depth):
        j = i // 2
        g1, b1, g2, b2 = ln_w[i]
        if i % 2 == 0:
            w = mla_w[j]
            q2d, k2d, v2d = mla_proj(x2d, seq, w, tabs)
            o2d = attention(q2d, k2d.T, v2d, batch, seq)
            x1, x1b = proj_res_ln(o2d, x2d, w["wo"], g1, b1, alpha)
        else:
            w = hgrn_w[j]
            proj = matmul(x2d, w["win"])
            o2 = hgrn_scan(proj, lbs[:, j], batch, seq)
            x1, x1b = hgrn_out(o2, proj, x2d, w["nw"], w["wo"], g1, b1, alpha)
        pw = peer_w_[i]
        gate, eid = peer_topk(x1b, pw["wq"], pw["keys"])
        wmat = peer_w(gate, eid)
        f2d = peer_dense(x1b, pw["ut"], pw["v"], wmat)
        wg, bg, wp = ple_w[i]
        x2d = ln_ple(x1, f2d, p[i].reshape(T, PLE_DIM), g2, b2, wg, bg, wp, alpha)
    return x2d.reshape(batch, seq, D_MODEL)


def kernel(x_prompt, x_sample, p_prompt, p_sample, mla_w_a, mla_q_norm, mla_kv_norm, mla_w_uq, mla_w_ukv, mla_w_o, hgrn_w_in, hgrn_lb, hgrn_norm, hgrn_w_o, peer_w_q, peer_sub_keys, peer_u, peer_v, ln1_g, ln1_b, ln2_g, ln2_b, ple_gate_w, ple_gate_b, ple_proj):
    depth = peer_w_q.shape[0]
    lbs = lower_bounds(hgrn_lb)
    mla_w = []
    for j in range(mla_w_a.shape[0]):
        w = _mla_weights(mla_w_a[j], mla_q_norm[j], mla_kv_norm[j], mla_w_uq[j], mla_w_ukv[j])
        w["wo"] = mla_w_o[j].astype(BF16)
        mla_w.append(w)
    hgrn_w = [{"win": hgrn_w_in[j].astype(BF16), "nw": hgrn_norm[j].astype(F32).reshape(1, -1),
               "wo": hgrn_w_o[j].astype(BF16)} for j in range(hgrn_w_in.shape[0])]
    peer_w_ = [{"wq": peer_w_q[i].astype(BF16), "keys": peer_sub_keys[i].astype(BF16),
                "ut": peer_u[i].astype(BF16).T, "v": peer_v[i].astype(BF16)} for i in range(depth)]
    r = lambda a: a.astype(F32).reshape(1, -1)
    ln_w = [(r(ln1_g[i]), r(ln1_b[i]), r(ln2_g[i]), r(ln2_b[i])) for i in range(depth)]
    ple_w = [(ple_gate_w[i].astype(BF16), r(ple_gate_b[i]), ple_proj[i].astype(BF16)) for i in range(depth)]
    args = (lbs, mla_w, hgrn_w, peer_w_, ln_w, ple_w, depth)
    return (_trunk(x_prompt, p_prompt, *args), _trunk(x_sample, p_sample, *args))
```

```python
import functools
import math

import numpy as np
import jax
import jax.numpy as jnp
from jax import lax
from jax.experimental import pallas as pl
from jax.experimental.pallas import tpu as pltpu

D_MODEL = 1024
PLE_DIM = 256
MLA_HEADS = 8
MLA_NOPE = 128
MLA_ROPE = 64
MLA_V = 128
MLA_Q_LORA = 384
MLA_KV_LORA = 256
MLA_SCALE = (MLA_NOPE + MLA_ROPE) ** -0.5
MLA_QK_PAD = 256
LOG2E = math.log2(math.e)
ROPE_THETA = 10000.0
HGRN_HEADS = 8
HGRN_DK = 128
HGRN_DV = 128
GATE_FLOOR = 1e-30
PEER_HEADS = 8
PEER_DK = 256
N_KEYS = 128
N_EXPERTS = N_KEYS * N_KEYS
PEER_TOPK = 16
LN_EPS = 1e-5
RMS_EPS = 1e-6

LANES = 128
VMEM_LIMIT = 56 * 1024 * 1024

SCAN_CHUNK = 128
SCAN_LEVELS = 7
W_PAD = 4
W_PITCH = N_KEYS + W_PAD
W_LHS_PAD = 16
W_GROUP = 32
W_ROWS = 8
FUSE_LANES = 128

F32 = jnp.float32
BF16 = jnp.bfloat16

_NT = (((1,), (1,)), ((), ()))


def _params(*sem):
    return pltpu.CompilerParams(dimension_semantics=sem, vmem_limit_bytes=VMEM_LIMIT)


def _dot(a, b):
    return jnp.dot(a, b, preferred_element_type=F32)


def _dot_nt(a, b):
    return lax.dot_general(a, b, _NT, preferred_element_type=F32)


def _layer_norm(y, g, b):
    mu = jnp.mean(y, axis=-1, keepdims=True)
    yc = y - mu
    var = jnp.mean(yc * yc, axis=-1, keepdims=True)
    return yc * lax.rsqrt(var + LN_EPS) * g + b


def _silu(t):
    return t * (1.0 / (1.0 + jnp.exp(-t)))


def _full(shape):
    nd = len(shape)
    return pl.BlockSpec(shape, lambda *_: (0,) * nd)


def _lower_bounds_kernel(lb_ref, out_ref):
    n = lb_ref.shape[1]
    for d in range(lb_ref.shape[0]):
        rows = [lb_ref[d, j] for j in range(n)]
        m = rows[0]
        for r in rows[1:]:
            m = jnp.maximum(m, r)
        ex = [jnp.exp(r - m) for r in rows]
        tot = ex[0]
        for e in ex[1:]:
            tot = tot + e
        sm = [e / tot for e in ex]
        cum = sm[0]
        out_ref[d, 0] = cum - sm[0]
        for j in range(1, n):
            cum = cum + sm[j]
            out_ref[d, j] = cum - sm[0]


def lower_bounds(hgrn_lb):
    two, n, hk = hgrn_lb.shape
    lb4 = hgrn_lb.astype(F32).reshape(two, n, 1, hk)
    out = pl.pallas_call(
        _lower_bounds_kernel,
        out_shape=jax.ShapeDtypeStruct((two, n, 1, hk), F32),
        name="hgrn_lower_bounds",
    )(lb4)
    return out


def _mla_proj_kernel(x_ref, wa_ref, qn_ref, kvn_ref, wuq_ref, wuqr_ref, wuk_ref, wuv_ref,
                     cq_ref, sq_ref, ck_ref, sk_ref, q_ref, k_ref, v_ref):
    xb = x_ref[...].astype(BF16)
    a = _dot(xb, wa_ref[...])
    c_q = a[:, :MLA_Q_LORA]
    c_kv = a[:, MLA_Q_LORA:MLA_Q_LORA + MLA_KV_LORA]
    kr = a[:, 640:768]
    krr = a[:, 768:896]
    cqn = (c_q * lax.rsqrt(jnp.mean(c_q * c_q, axis=-1, keepdims=True) + RMS_EPS) * qn_ref[...]).astype(BF16)
    ckvn = (c_kv * lax.rsqrt(jnp.mean(c_kv * c_kv, axis=-1, keepdims=True) + RMS_EPS) * kvn_ref[...]).astype(BF16)
    q = _dot(cqn, wuq_ref[...])
    qr = _dot(cqn, wuqr_ref[...])
    cq = cq_ref[...]
    sq = sq_ref[...]
    kro = (kr * ck_ref[...] + krr * sk_ref[...]).astype(BF16)
    kn = _dot(ckvn, wuk_ref[...]).astype(BF16)
    v_ref[...] = _dot(ckvn, wuv_ref[...]).astype(BF16)
    for h in range(MLA_HEADS):
        lo = h * MLA_QK_PAD
        qh = (q[:, lo:lo + MLA_QK_PAD] * cq + qr[:, lo:lo + MLA_QK_PAD] * sq) * (MLA_SCALE * LOG2E)
        q_ref[:, lo:lo + MLA_QK_PAD] = qh.astype(BF16)
        k_ref[:, lo:lo + MLA_NOPE] = kn[:, h * MLA_NOPE:(h + 1) * MLA_NOPE]
        k_ref[:, lo + MLA_NOPE:lo + MLA_QK_PAD] = kro


def mla_proj(x2d, seq, w, tabs, tm=256):
    T = x2d.shape[0]
    tm = math.gcd(seq, tm)
    nper = seq // tm
    cq, sq, ck, sk = tabs
    row = lambda i: (i, 0)
    pos = lambda i: (i % nper, 0)
    hq = MLA_HEADS * MLA_QK_PAD
    return pl.pallas_call(
        _mla_proj_kernel,
        grid=(T // tm,),
        in_specs=[
            pl.BlockSpec((tm, D_MODEL), row),
            _full(w["wa"].shape), _full(w["qn"].shape), _full(w["kvn"].shape),
            _full(w["wuq"].shape), _full(w["wuqr"].shape), _full(w["wuk"].shape), _full(w["wuv"].shape),
            pl.BlockSpec((tm, MLA_QK_PAD), pos), pl.BlockSpec((tm, MLA_QK_PAD), pos),
            pl.BlockSpec((tm, LANES), pos), pl.BlockSpec((tm, LANES), pos),
        ],
        out_specs=[pl.BlockSpec((tm, hq), row), pl.BlockSpec((tm, hq), row),
                   pl.BlockSpec((tm, MLA_HEADS * MLA_V), row)],
        out_shape=[jax.ShapeDtypeStruct((T, hq), BF16), jax.ShapeDtypeStruct((T, hq), BF16),
                   jax.ShapeDtypeStruct((T, MLA_HEADS * MLA_V), BF16)],
        compiler_params=_params("parallel"),
        name="mla_proj",
    )(x2d, w["wa"], w["qn"], w["kvn"], w["wuq"], w["wuqr"], w["wuk"], w["wuv"], cq, sq, ck, sk)


def _attn_kernel(q_ref, kt_ref, v_ref, o_ref, *, tk, unroll):
    q = q_ref[...]
    tq = q.shape[0]
    nk = kt_ref.shape[1] // tk

    def body(c, carry):
        m, l, acc = carry
        for u in range(unroll):
            off = pl.multiple_of((c * unroll + u) * tk, tk)
            s = _dot(q, kt_ref[:, pl.ds(off, tk)])
            m_new = jnp.maximum(m, jnp.max(s, axis=-1, keepdims=True))
            corr = jnp.exp2(m - m_new)
            p = jnp.exp2(s - m_new)
            l = corr * l + jnp.sum(p, axis=-1, keepdims=True)
            acc = corr * acc + _dot(p.astype(BF16), v_ref[pl.ds(off, tk), :])
            m = m_new
        return m, l, acc

    init = (jnp.full((tq, 1), -jnp.inf, F32), jnp.zeros((tq, 1), F32), jnp.zeros((tq, MLA_V), F32))
    _, l, acc = lax.fori_loop(0, nk // unroll, body, init)
    o_ref[...] = (acc / l).astype(o_ref.dtype)


def attention(q2d, kt2d, v2d, batch, seq, tq=512, tk=512, unroll=4):
    T = q2d.shape[0]
    tq, tk = math.gcd(seq, tq), math.gcd(seq, tk)
    unroll = math.gcd(seq // tk, unroll)
    nq = seq // tq
    return pl.pallas_call(
        functools.partial(_attn_kernel, tk=tk, unroll=unroll),
        grid=(batch, MLA_HEADS, nq),
        in_specs=[
            pl.BlockSpec((tq, MLA_QK_PAD), lambda b, h, i: (b * nq + i, h)),
            pl.BlockSpec((MLA_QK_PAD, seq), lambda b, h, i: (h, b)),
            pl.BlockSpec((seq, MLA_V), lambda b, h, i: (b, h)),
        ],
        out_specs=pl.BlockSpec((tq, MLA_V), lambda b, h, i: (b * nq + i, h)),
        out_shape=jax.ShapeDtypeStruct((T, MLA_HEADS * MLA_V), BF16),
        compiler_params=_params("parallel", "parallel", "arbitrary"),
        name="mla_attention",
    )(q2d, kt2d, v2d)


def _proj_res_ln_kernel(a_ref, x_ref, w_ref, g_ref, b_ref, o_ref, ob_ref, *, alpha):
    h = _dot(a_ref[...], w_ref[...])
    y = _layer_norm(alpha * x_ref[...] + h, g_ref[...], b_ref[...])
    o_ref[...] = y
    ob_ref[...] = y.astype(BF16)


def proj_res_ln(a2d, x2d, w, g, b, alpha, tm=512):
    T = x2d.shape[0]
    tm = math.gcd(T, tm)
    row = lambda i: (i, 0)
    return pl.pallas_call(
        functools.partial(_proj_res_ln_kernel, alpha=alpha),
        grid=(T // tm,),
        in_specs=[pl.BlockSpec((tm, a2d.shape[1]), row), pl.BlockSpec((tm, D_MODEL), row),
                  _full(w.shape), _full(g.shape), _full(b.shape)],
        out_specs=[pl.BlockSpec((tm, D_MODEL), row), pl.BlockSpec((tm, D_MODEL), row)],
        out_shape=[jax.ShapeDtypeStruct((T, D_MODEL), F32), jax.ShapeDtypeStruct((T, D_MODEL), BF16)],
        compiler_params=_params("parallel"),
        name="proj_res_ln",
    )(a2d, x2d, w, g, b)


def _matmul_kernel(x_ref, w_ref, o_ref):
    o_ref[...] = _dot(x_ref[...].astype(BF16), w_ref[...])


def matmul(x2d, w, tm=512, tn=1280):
    T, K = x2d.shape
    N = w.shape[1]
    tm = math.gcd(T, tm)
    return pl.pallas_call(
        _matmul_kernel,
        grid=(T // tm, N // tn),
        in_specs=[pl.BlockSpec((tm, K), lambda i, j: (i, 0)), pl.BlockSpec((K, tn), lambda i, j: (0, j))],
        out_specs=pl.BlockSpec((tm, tn), lambda i, j: (i, j)),
        out_shape=jax.ShapeDtypeStruct((T, N), F32),
        compiler_params=_params("parallel", "arbitrary"),
        name="matmul",
    )(x2d, w)


def _scan_tables():
    C, L = SCAN_CHUNK, SCAN_LEVELS
    mats = np.zeros((2, (2 + L) * C, C), np.float32)
    lvl = np.full((2, C, C), -1, np.int32)
    for d in range(2):
        p = np.arange(C) if d == 0 else C - 1 - np.arange(C)
        pt, pu = p[:, None], p[None, :]
        mats[d, 0:C] = pu <= pt
        mats[d, C:2 * C] = pu > pt
        for l in range(L):
            m = 1 << l
            r = (pt // (2 * m)) * (2 * m) + m - 1
            qside = (pt % (2 * m)) >= m
            e = np.where(qside, (pu > r) & (pu <= pt), (pu > pt) & (pu <= r))
            mats[d, (2 + l) * C:(3 + l) * C] = e
            same = (pt // (2 * m)) == (pu // (2 * m))
            lvl[d][same & qside & ((pu % (2 * m)) < m)] = l
        lvl[d][pt == pu] = L
    return jnp.asarray(np.concatenate([mats, mats], axis=2), BF16), jnp.asarray(np.concatenate([lvl, lvl], axis=2))


def _hgrn_scan_kernel(q_ref, z_ref, v_ref, lb_ref, mat_ref, lvl_ref, o_ref, state_ref, kk_ref, ex_ref):
    C, L = SCAN_CHUNK, SCAN_LEVELS

    @pl.when(pl.program_id(2) == 0)
    def _():
        state_ref[...] = jnp.zeros_like(state_ref)

    z = z_ref[...]
    lb = lb_ref[...]
    e = jnp.exp(-jnp.abs(z))
    r = 1.0 / (1.0 + e)
    er = e * r
    pos = z >= 0
    f = lb + (1.0 - lb) * jnp.where(pos, r, er)
    g = jnp.log(jnp.maximum(f, GATE_FLOOR))
    kk_ref[...] = (1.0 - lb) * jnp.where(pos, er, r)
    g1 = g.astype(BF16)
    g2 = (g - g1.astype(F32)).astype(BF16)
    ex_ref[...] = _dot(mat_ref[...], jnp.concatenate([g1, g2], axis=0))

    lvl = lvl_ref[...]
    zero = jnp.zeros((C, HGRN_DK), BF16)
    wide = 2 * HGRN_DK

    def block_diag(x):
        return jnp.concatenate([jnp.concatenate([x[:, :HGRN_DK], zero], axis=1),
                                jnp.concatenate([zero, x[:, HGRN_DK:]], axis=1)], axis=0)

    for hp in range(HGRN_HEADS // 2):
        sl = slice(hp * wide, (hp + 1) * wide)
        q = _silu(q_ref[:, sl])
        kk = kk_ref[:, sl]
        qb = q.astype(BF16)
        kb = kk.astype(BF16)
        attn = jnp.where(lvl == L, _dot_nt(qb, block_diag(kb)), 0.0)
        for l in range(L):
            xl = jnp.exp(ex_ref[(2 + l) * C:(3 + l) * C, sl]).astype(BF16)
            p = _dot_nt(qb * xl, block_diag(kb * xl))
            attn = jnp.where(lvl == l, p, attn)
        v2 = v_ref[:, sl]
        e_b = ex_ref[0:C, sl]
        e_last = ex_ref[C:2 * C, sl]
        b_last = e_b[0:1] + e_last[0:1]
        st = [state_ref[2 * hp + d] for d in range(2)]
        st_bd = jnp.concatenate([jnp.concatenate([st[0].astype(BF16), zero], axis=1),
                                 jnp.concatenate([zero, st[1].astype(BF16)], axis=1)], axis=0)
        o = _dot(attn.astype(BF16), block_diag(v2.astype(BF16)))
        o_ref[:, sl] = o + _dot_nt((q * jnp.exp(e_b)).astype(BF16), st_bd)
        kd = (kk * jnp.exp(e_last)).astype(BF16)
        for d in range(2):
            hs = slice(d * HGRN_DK, (d + 1) * HGRN_DK)
            vt = v2[:, hs].T.astype(BF16)
            state_ref[2 * hp + d] = st[d] * jnp.exp(b_last[:, hs]) + _dot(vt, kd[:, hs])


def hgrn_scan(proj, lb, batch, seq):
    T = proj.shape[0]
    C = SCAN_CHUNK
    nc = seq // C
    hk = HGRN_HEADS * HGRN_DK
    mats, lvl = _scan_tables()

    def rowblk(b, d, c):
        return b * nc + c + d * (nc - 1 - 2 * c)

    return pl.pallas_call(
        _hgrn_scan_kernel,
        grid=(batch, 2, nc),
        in_specs=[
            pl.BlockSpec((C, hk), lambda b, d, c: (rowblk(b, d, c), 0)),
            pl.BlockSpec((C, hk), lambda b, d, c: (rowblk(b, d, c), 1 + d)),
            pl.BlockSpec((C, hk), lambda b, d, c: (rowblk(b, d, c), 3)),
            pl.BlockSpec((None, 1, hk), lambda b, d, c: (d, 0, 0)),
            pl.BlockSpec((None,) + mats.shape[1:], lambda b, d, c: (d, 0, 0)),
            pl.BlockSpec((None, C, 2 * C), lambda b, d, c: (d, 0, 0)),
        ],
        out_specs=pl.BlockSpec((None, C, hk), lambda b, d, c: (d, rowblk(b, d, c), 0)),
        out_shape=jax.ShapeDtypeStruct((2, T, hk), F32),
        scratch_shapes=[pltpu.VMEM((HGRN_HEADS, HGRN_DV, HGRN_DK), F32), pltpu.VMEM((C, hk), F32),
                        pltpu.VMEM(((2 + SCAN_LEVELS) * C, hk), F32)],
        compiler_params=_params("parallel", "arbitrary", "arbitrary"),
        name="hgrn_scan",
    )(proj, proj, proj, lb, mats, lvl)


def _hgrn_out_kernel(o_ref, g_ref, x_ref, nw_ref, w_ref, lg_ref, lbias_ref, y_ref, yb_ref, a_ref, *, alpha):
    nw = nw_ref[...]
    for h in range(HGRN_HEADS):
        sl = slice(h * HGRN_DV, (h + 1) * HGRN_DV)
        o = o_ref[0, :, sl] + o_ref[1, :, sl]
        on = o * lax.rsqrt(jnp.mean(o * o, axis=-1, keepdims=True) + RMS_EPS) * nw
        a_ref[:, sl] = (on * _silu(g_ref[:, sl])).astype(BF16)
    hmix = _dot(a_ref[...], w_ref[...])
    y = _layer_norm(alpha * x_ref[...] + hmix, lg_ref[...], lbias_ref[...])
    y_ref[...] = y
    yb_ref[...] = y.astype(BF16)


def hgrn_out(o2, proj, x2d, nw, w, g, b, alpha, tm=256):
    T = x2d.shape[0]
    tm = math.gcd(T, tm)
    hv = HGRN_HEADS * HGRN_DV
    row = lambda i: (i, 0)
    return pl.pallas_call(
        functools.partial(_hgrn_out_kernel, alpha=alpha),
        grid=(T // tm,),
        in_specs=[pl.BlockSpec((2, tm, hv), lambda i: (0, i, 0)),
                  pl.BlockSpec((tm, hv), lambda i: (i, 4)),
                  pl.BlockSpec((tm, D_MODEL), row),
                  _full(nw.shape), _full(w.shape), _full(g.shape), _full(b.shape)],
        out_specs=[pl.BlockSpec((tm, D_MODEL), row), pl.BlockSpec((tm, D_MODEL), row)],
        out_shape=[jax.ShapeDtypeStruct((T, D_MODEL), F32), jax.ShapeDtypeStruct((T, D_MODEL), BF16)],
        scratch_shapes=[pltpu.VMEM((tm, hv), BF16)],
        compiler_params=_params("parallel"),
        name="hgrn_out",
    )(o2, proj, x2d, nw, w, g, b)


_PAIRS = [(a, b) for a in range(PEER_TOPK) for b in range(PEER_TOPK) if (a + 1) * (b + 1) <= PEER_TOPK]
_NPAIR_PAD = -(-len(_PAIRS) // 8) * 8


def _top_rows(s, k, payload=None):
    n = s.shape[0]
    iota = lax.broadcasted_iota(jnp.int32, s.shape, 0)
    sub = 8
    iota8 = lax.broadcasted_iota(jnp.int32, (sub, s.shape[1]), 0)
    vals, idxs = [], []
    for _ in range(k):
        nodes = [(s[r:r + sub], iota8 + r) for r in range(0, n, sub)]
        while len(nodes) > 1:
            nxt = [(jnp.maximum(va, vb), jnp.where(va >= vb, ia, ib))
                   for (va, ia), (vb, ib) in zip(nodes[0::2], nodes[1::2])]
            nodes = nxt + ([nodes[-1]] if len(nodes) % 2 else [])
        v8, i8 = nodes[0]
        m = jnp.max(v8, axis=0, keepdims=True)
        im = jnp.min(jnp.where(v8 == m, i8, n), axis=0, keepdims=True)
        hit = iota == im
        vals.append(m)
        if payload is None:
            idxs.append(im)
        else:
            idxs.append(jnp.max(jnp.where(hit, payload, -1), axis=0, keepdims=True))
        s = jnp.where(hit, -jnp.inf, s)
    return vals, idxs


def _retrieve_head(q_halves, keys_ref):
    sv, si = [], []
    for c in range(2):
        s = _dot_nt(keys_ref[c], q_halves[c])
        v_, i_ = _top_rows(s, PEER_TOPK)
        sv.append(v_)
        si.append(i_)
    return _combine_halves(sv, si)


def _combine_halves(sv, si):
    tm = sv[0][0].shape[1]
    neg = jnp.full((_NPAIR_PAD - len(_PAIRS), tm), -jnp.inf, F32)
    zero = jnp.zeros((_NPAIR_PAD - len(_PAIRS), tm), jnp.int32)
    cand = jnp.concatenate([sv[0][a] + sv[1][b] for a, b in _PAIRS] + [neg], axis=0)
    cid = jnp.concatenate([si[0][a] * N_KEYS + si[1][b] for a, b in _PAIRS] + [zero], axis=0)
    tv, te = _top_rows(cand, PEER_TOPK, payload=cid)
    tv = jnp.concatenate(tv, axis=0)
    ex = jnp.exp(tv - tv[0:1])
    return ex / jnp.sum(ex, axis=0, keepdims=True), jnp.concatenate(te, axis=0)


def _peer_topk_kernel(xb_ref, wq_ref, keys_ref, gate_ref, eid_ref, q_ref):
    q_ref[...] = _dot(xb_ref[...], wq_ref[...]).astype(BF16)
    half = PEER_DK // 2

    def head(h, carry):
        off = pl.multiple_of(h * PEER_DK, PEER_DK)
        gate, eid = _retrieve_head([q_ref[:, pl.ds(off + c * half, half)] for c in range(2)], keys_ref)
        row = pl.multiple_of(h * PEER_TOPK, PEER_TOPK)
        gate_ref[pl.ds(row, PEER_TOPK), :] = gate
        eid_ref[pl.ds(row, PEER_TOPK), :] = eid
        return carry

    lax.fori_loop(0, PEER_HEADS, head, 0)


def peer_topk(xb, wq, keys, tm=256):
    T = xb.shape[0]
    tm = math.gcd(T, tm)
    ns = PEER_HEADS * PEER_TOPK
    return pl.pallas_call(
        _peer_topk_kernel,
        grid=(T // tm,),
        in_specs=[pl.BlockSpec((tm, D_MODEL), lambda i: (i, 0)), _full(wq.shape), _full(keys.shape)],
        out_specs=[pl.BlockSpec((ns, tm), lambda i: (0, i)), pl.BlockSpec((ns, tm), lambda i: (0, i))],
        out_shape=[jax.ShapeDtypeStruct((ns, T), F32), jax.ShapeDtypeStruct((ns, T), jnp.int32)],
        scratch_shapes=[pltpu.VMEM((tm, PEER_HEADS * PEER_DK), BF16)],
        compiler_params=_params("parallel"),
        name="peer_topk",
    )(xb, wq, keys)


def _peer_w_kernel(gate_ref, eid_ref, w_ref, gt_ref, it_ref, jt_ref, tile_ref):
    tm = gate_ref.shape[1]
    eid_t = eid_ref[...].T
    gt_ref[...] = gate_ref[...].T
    it_ref[...] = lax.shift_right_logical(eid_t, int(math.log2(N_KEYS)))
    jt_ref[...] = lax.bitwise_and(eid_t, N_KEYS - 1)
    ns = gate_ref.shape[0]
    iota = lax.broadcasted_iota(jnp.int32, (N_KEYS, ns), 0)
    iota_pad = lax.broadcasted_iota(jnp.int32, (N_KEYS + W_LHS_PAD, ns), 0)
    zero = jnp.zeros((N_KEYS, ns), BF16)

    def token_group(p, carry):
        base = pl.multiple_of(p * W_GROUP, W_GROUP)
        g8 = gt_ref[pl.ds(base, W_GROUP), :]
        i8 = it_ref[pl.ds(base, W_GROUP), :]
        j8 = jt_ref[pl.ds(base, W_GROUP), :]
        for u in range(0, W_GROUP, 2):
            at0 = jnp.where(iota_pad == i8[u:u + 1], g8[u:u + 1], 0.0).astype(BF16)
            at1 = jnp.where(iota_pad == i8[u + 1:u + 2] + W_PAD, g8[u + 1:u + 2], 0.0).astype(BF16)
            bt = [jnp.where(iota == j8[u + d:u + d + 1], 1.0, 0.0).astype(BF16) for d in range(2)]
            lhs = jnp.concatenate([at0, at1], axis=1)
            rhs = jnp.concatenate([jnp.concatenate([bt[0], zero], axis=1),
                                   jnp.concatenate([zero, bt[1]], axis=1)], axis=0)
            res = _dot_nt(lhs, rhs)
            row = pl.multiple_of((base + u) * W_PITCH, 8)
            tile_ref[pl.ds(row, N_KEYS), :] = res[:N_KEYS, :N_KEYS]
            tile_ref[pl.ds(row + N_KEYS, N_KEYS + 2 * W_PAD), :] = res[:N_KEYS + 2 * W_PAD, N_KEYS:]
        return carry

    lax.fori_loop(0, tm // W_GROUP, token_group, 0)

    def key_rows(i2, carry):
        for d in range(W_ROWS):
            i = i2 * W_ROWS + d
            col = tile_ref[pl.ds(i, tm, stride=W_PITCH), :]
            w_ref[:, pl.ds(pl.multiple_of(i * N_KEYS, N_KEYS), N_KEYS)] = col.astype(BF16)
        return carry

    lax.fori_loop(0, N_KEYS // W_ROWS, key_rows, 0)


def peer_w(gate, eid, tm=128):
    ns, T = gate.shape
    tm = math.gcd(T, tm)
    return pl.pallas_call(
        _peer_w_kernel,
        grid=(T // tm,),
        in_specs=[pl.BlockSpec((ns, tm), lambda i: (0, i)), pl.BlockSpec((ns, tm), lambda i: (0, i))],
        out_specs=pl.BlockSpec((tm, N_EXPERTS), lambda i: (i, 0)),
        out_shape=jax.ShapeDtypeStruct((T, N_EXPERTS), BF16),
        scratch_shapes=[pltpu.VMEM((tm, ns), F32), pltpu.VMEM((tm, ns), jnp.int32),
                        pltpu.VMEM((tm, ns), jnp.int32), pltpu.VMEM((tm * W_PITCH, N_KEYS), F32)],
        compiler_params=_params("parallel"),
        name="peer_gate_matrix",
    )(gate, eid)


def _dense_init(o_ref):
    @pl.when(pl.program_id(1) == 0)
    def _():
        o_ref[...] = jnp.zeros_like(o_ref)


def _dense_step(xb_ref, ut_ref, v_ref, w_ref, o_ref):
    h = _dot(xb_ref[...], ut_ref[...])
    act = 0.5 * h * (1.0 + lax.erf(h * (1.0 / math.sqrt(2.0))))
    hw = act * w_ref[...].astype(F32)
    o_ref[...] += _dot(hw.astype(BF16), v_ref[...])
    return hw


def _after(x, marker):
    z = pltpu.bitcast(marker, jnp.uint32)
    z = lax.shift_right_logical(lax.shift_right_logical(z, jnp.uint32(31)), jnp.uint32(1))
    z = pltpu.bitcast(z, F32)
    return x + jnp.tile(z, (x.shape[0] // z.shape[0], x.shape[1] // z.shape[1]))


def _peer_dense_kernel(xb_ref, ut_ref, v_ref, w_ref, o_ref):
    _dense_init(o_ref)
    _dense_step(xb_ref, ut_ref, v_ref, w_ref, o_ref)


def _peer_dense_topk_kernel(xb_ref, ut_ref, v_ref, w_ref, xn_ref, wq_ref, keys_ref, o_ref, gate_ref, eid_ref):
    _dense_init(o_ref)
    half = PEER_DK // 2
    q = _dot(xn_ref[...], wq_ref[...]).astype(BF16)
    scores = [_dot_nt(keys_ref[c], q[:, c * half:(c + 1) * half]) for c in range(2)]
    hw = _dense_step(xb_ref, ut_ref, v_ref, w_ref, o_ref)
    marker = hw[hw.shape[0] - 8:, hw.shape[1] - LANES:]
    row = pl.multiple_of(pl.program_id(1) * PEER_TOPK, PEER_TOPK)
    tm = xn_ref.shape[0]
    sub = math.gcd(tm, FUSE_LANES)
    for t0 in range(0, tm, sub):
        top = [_top_rows(_after(scores[c][:, t0:t0 + sub], marker), PEER_TOPK) for c in range(2)]
        gate, eid = _combine_halves([top[0][0], top[1][0]], [top[0][1], top[1][1]])
        gate_ref[pl.ds(row, PEER_TOPK), t0:t0 + sub] = gate
        eid_ref[pl.ds(row, PEER_TOPK), t0:t0 + sub] = eid


def _dense_specs(tm, ne):
    return [pl.BlockSpec((tm, D_MODEL), lambda i, j: (i, 0)),
            pl.BlockSpec((D_MODEL, ne), lambda i, j: (0, j)),
            pl.BlockSpec((ne, D_MODEL), lambda i, j: (j, 0)),
            pl.BlockSpec((tm, ne), lambda i, j: (i, j))]


def peer_dense(xb, ut, v, wmat, tm=512, ne=N_EXPERTS // PEER_HEADS):
    T = xb.shape[0]
    tm = math.gcd(T, tm)
    return pl.pallas_call(
        _peer_dense_kernel,
        grid=(T // tm, N_EXPERTS // ne),
        in_specs=_dense_specs(tm, ne),
        out_specs=pl.BlockSpec((tm, D_MODEL), lambda i, j: (i, 0)),
        out_shape=jax.ShapeDtypeStruct((T, D_MODEL), F32),
        compiler_params=_params("parallel", "arbitrary"),
        name="peer_dense",
    )(xb, ut, v, wmat)


def peer_dense_topk(xb, ut, v, wmat, xn, wq, keys, tm=512):
    T = xb.shape[0]
    assert xn.shape[0] == T
    tm = math.gcd(T, tm)
    ne = N_EXPERTS // PEER_HEADS
    ns = PEER_HEADS * PEER_TOPK
    return pl.pallas_call(
        _peer_dense_topk_kernel,
        grid=(T // tm, PEER_HEADS),
        in_specs=_dense_specs(tm, ne) + [
            pl.BlockSpec((tm, D_MODEL), lambda i, j: (i, 0)),
            pl.BlockSpec((D_MODEL, PEER_DK), lambda i, j: (0, j)),
            _full(keys.shape)],
        out_specs=[pl.BlockSpec((tm, D_MODEL), lambda i, j: (i, 0)),
                   pl.BlockSpec((ns, tm), lambda i, j: (0, i)), pl.BlockSpec((ns, tm), lambda i, j: (0, i))],
        out_shape=[jax.ShapeDtypeStruct((T, D_MODEL), F32),
                   jax.ShapeDtypeStruct((ns, T), F32), jax.ShapeDtypeStruct((ns, T), jnp.int32)],
        compiler_params=_params("parallel", "arbitrary"),
        name="peer_dense_topk",
    )(xb, ut, v, wmat, xn, wq, keys)


def _ln_ple_kernel(x_ref, f_ref, p_ref, g_ref, b_ref, wg_ref, bg_ref, wp_ref, o_ref, *, alpha):
    y = _layer_norm(alpha * x_ref[...] + f_ref[...], g_ref[...], b_ref[...])
    gate = 1.0 / (1.0 + jnp.exp(-(_dot(y.astype(BF16), wg_ref[...]) + bg_ref[...])))
    o_ref[...] = y + gate * _dot(p_ref[...].astype(BF16), wp_ref[...])


def ln_ple(x2d, f2d, p2d, p_row0, g, b, wg, bg, wp, alpha, tm=512):
    T = x2d.shape[0]
    tm = math.gcd(math.gcd(T, tm), p_row0) if p_row0 else math.gcd(T, tm)
    blk0 = p_row0 // tm
    row = lambda i: (i, 0)
    return pl.pallas_call(
        functools.partial(_ln_ple_kernel, alpha=alpha),
        grid=(T // tm,),
        in_specs=[pl.BlockSpec((tm, D_MODEL), row), pl.BlockSpec((tm, D_MODEL), row),
                  pl.BlockSpec((tm, PLE_DIM), lambda i: (i + blk0, 0)),
                  _full(g.shape), _full(b.shape), _full(wg.shape), _full(bg.shape), _full(wp.shape)],
        out_specs=pl.BlockSpec((tm, D_MODEL), row),
        out_shape=jax.ShapeDtypeStruct((T, D_MODEL), F32),
        compiler_params=_params("parallel"),
        name="ln_ple",
    )(x2d, f2d, p2d, g, b, wg, bg, wp)


def _rot_cols(w):
    half = MLA_ROPE // 2
    return jnp.concatenate([-w[..., half:], w[..., :half]], axis=-1)


def _mla_weights(w_a, q_norm, kv_norm, w_uq, w_ukv):
    zk = jnp.zeros((D_MODEL, LANES - MLA_ROPE), F32)
    w_kr = w_a[:, MLA_Q_LORA + MLA_KV_LORA:]
    wa = jnp.concatenate([w_a[:, :MLA_Q_LORA + MLA_KV_LORA], w_kr, zk, _rot_cols(w_kr), zk], axis=1)
    uq = w_uq.reshape(MLA_Q_LORA, MLA_HEADS, MLA_NOPE + MLA_ROPE)
    zq = jnp.zeros((MLA_Q_LORA, MLA_HEADS, MLA_QK_PAD - MLA_NOPE - MLA_ROPE), F32)
    wuq = jnp.concatenate([uq, zq], axis=-1).reshape(MLA_Q_LORA, MLA_HEADS * MLA_QK_PAD)
    wuqr = jnp.concatenate([jnp.zeros_like(uq[..., :MLA_NOPE]), _rot_cols(uq[..., MLA_NOPE:]), zq],
                           axis=-1).reshape(MLA_Q_LORA, MLA_HEADS * MLA_QK_PAD)
    ukv = w_ukv.reshape(MLA_KV_LORA, MLA_HEADS, MLA_NOPE + MLA_V)
    return {
        "wa": wa.astype(BF16),
        "qn": q_norm.astype(F32).reshape(1, -1),
        "kvn": kv_norm.astype(F32).reshape(1, -1),
        "wuq": wuq.astype(BF16),
        "wuqr": wuqr.astype(BF16),
        "wuk": ukv[..., :MLA_NOPE].reshape(MLA_KV_LORA, -1).astype(BF16),
        "wuv": ukv[..., MLA_NOPE:].reshape(MLA_KV_LORA, -1).astype(BF16),
    }


def _rope_tabs(seq):
    inv_freq = 1.0 / (ROPE_THETA ** (jnp.arange(0, MLA_ROPE, 2, dtype=F32) / MLA_ROPE))
    ang = jnp.arange(seq, dtype=F32)[:, None] * inv_freq[None, :]
    emb = jnp.concatenate([ang, ang], axis=-1)
    cos, sin = jnp.cos(emb), jnp.sin(emb)
    one = jnp.ones((seq, MLA_NOPE), F32)
    z64 = jnp.zeros((seq, LANES - MLA_ROPE), F32)
    z128 = jnp.zeros((seq, MLA_NOPE), F32)
    return (jnp.concatenate([one, cos, z64], axis=1), jnp.concatenate([z128, sin, z64], axis=1),
            jnp.concatenate([cos, z64], axis=1), jnp.concatenate([sin, z64], axis=1))


def _mixer(i, x2d, batch, seq, lbs, mla_w, hgrn_w, ln_w, alpha, tabs):
    j = i // 2
    g1, b1, _, _ = ln_w[i]
    if i % 2 == 0:
        w = mla_w[j]
        q2d, k2d, v2d = mla_proj(x2d, seq, w, tabs)
        o2d = attention(q2d, k2d.T, v2d, batch, seq)
        return proj_res_ln(o2d, x2d, w["wo"], g1, b1, alpha)
    w = hgrn_w[j]
    proj = matmul(x2d, w["win"])
    o2 = hgrn_scan(proj, lbs[:, j], batch, seq)
    return hgrn_out(o2, proj, x2d, w["nw"], w["wo"], g1, b1, alpha)


def _peer_groups(x1b, pw):
    n = len(x1b)
    if n > 1 and all(x.shape[0] == x1b[0].shape[0] for x in x1b):
        out = []
        sel = peer_topk(x1b[0], pw["wq"], pw["keys"])
        for k in range(n):
            wmat = peer_w(*sel)
            if k + 1 < n:
                f, gate, eid = peer_dense_topk(x1b[k], pw["ut"], pw["v"], wmat, x1b[k + 1], pw["wq"], pw["keys"])
                sel = (gate, eid)
            else:
                f = peer_dense(x1b[k], pw["ut"], pw["v"], wmat)
            out.append(f)
        return out
    return [peer_dense(x, pw["ut"], pw["v"], peer_w(*peer_topk(x, pw["wq"], pw["keys"]))) for x in x1b]


def _trunk(groups, lbs, mla_w, hgrn_w, peer_w_, ln_w, ple_w, depth):
    alpha = (2 * depth) ** 0.25
    tabs = {g["seq"]: None for g in groups}
    for s in tabs:
        tabs[s] = _rope_tabs(s)
    xs = [g["x"] for g in groups]
    for i in range(depth):
        mixed = [_mixer(i, x, g["batch"], g["seq"], lbs, mla_w, hgrn_w, ln_w, alpha, tabs[g["seq"]])
                 for x, g in zip(xs, groups)]
        fs = _peer_groups([m[1] for m in mixed], peer_w_[i])
        _, _, g2, b2 = ln_w[i]
        wg, bg, wp = ple_w[i]
        xs = [ln_ple(m[0], f, g["p"][i], g["row0"], g2, b2, wg, bg, wp, alpha)
              for m, f, g in zip(mixed, fs, groups)]
    return xs


def _token_groups(x, p, target):
    batch, seq, _ = x.shape
    depth = p.shape[0]
    T = batch * seq
    x2d = x.reshape(T, D_MODEL)
    p2d = p.reshape(depth, T, PLE_DIM)
    n = T // target if (target % seq == 0 and T % target == 0) else 1
    tg = T // n
    return [{"x": x2d if n == 1 else x2d[k * tg:(k + 1) * tg], "p": p2d, "row0": k * tg,
             "batch": batch // n, "seq": seq} for k in range(n)]


def kernel(x_prompt, x_sample, p_prompt, p_sample, mla_w_a, mla_q_norm, mla_kv_norm, mla_w_uq, mla_w_ukv, mla_w_o, hgrn_w_in, hgrn_lb, hgrn_norm, hgrn_w_o, peer_w_q, peer_sub_keys, peer_u, peer_v, ln1_g, ln1_b, ln2_g, ln2_b, ple_gate_w, ple_gate_b, ple_proj):
    depth = peer_w_q.shape[0]
    lbs = lower_bounds(hgrn_lb)
    mla_w = []
    for j in range(mla_w_a.shape[0]):
        w = _mla_weights(mla_w_a[j], mla_q_norm[j], mla_kv_norm[j], mla_w_uq[j], mla_w_ukv[j])
        w["wo"] = mla_w_o[j].astype(BF16)
        mla_w.append(w)
    hgrn_w = [{"win": hgrn_w_in[j].astype(BF16), "nw": hgrn_norm[j].astype(F32).reshape(1, -1),
               "wo": hgrn_w_o[j].astype(BF16)} for j in range(hgrn_w_in.shape[0])]
    peer_w_ = [{"wq": peer_w_q[i].astype(BF16), "keys": peer_sub_keys[i].astype(BF16),
                "ut": peer_u[i].astype(BF16).T, "v": peer_v[i].astype(BF16)} for i in range(depth)]
    r = lambda a: a.astype(F32).reshape(1, -1)
    ln_w = [(r(ln1_g[i]), r(ln1_b[i]), r(ln2_g[i]), r(ln2_b[i])) for i in range(depth)]
    ple_w = [(ple_gate_w[i].astype(BF16), r(ple_gate_b[i]), ple_proj[i].astype(BF16)) for i in range(depth)]
    target = x_prompt.shape[0] * x_prompt.shape[1]
    groups = _token_groups(x_prompt, p_prompt, target) + _token_groups(x_sample, p_sample, target)
    ys = _trunk(groups, lbs, mla_w, hgrn_w, peer_w_, ln_w, ple_w, depth)
    y_prompt = ys[0].reshape(x_prompt.shape)
    y_sample = (ys[1] if len(ys) == 2 else jnp.concatenate(ys[1:], axis=0)).reshape(x_sample.shape)
    return (y_prompt, y_sample)
```

```python
import functools
import math

import numpy as np
import jax
import jax.numpy as jnp
from jax import lax
from jax.experimental import pallas as pl
from jax.experimental.pallas import tpu as pltpu

D_MODEL = 1024
PLE_DIM = 256
MLA_HEADS = 8
MLA_NOPE = 128
MLA_ROPE = 64
MLA_V = 128
MLA_Q_LORA = 384
MLA_KV_LORA = 256
MLA_SCALE = (MLA_NOPE + MLA_ROPE) ** -0.5
MLA_QK_PAD = 256
LOG2E = math.log2(math.e)
ROPE_THETA = 10000.0
HGRN_HEADS = 8
HGRN_DK = 128
HGRN_DV = 128
GATE_FLOOR = 1e-30
PEER_HEADS = 8
PEER_DK = 256
N_KEYS = 128
N_EXPERTS = N_KEYS * N_KEYS
PEER_TOPK = 16
LN_EPS = 1e-5
RMS_EPS = 1e-6

LANES = 128
VMEM_LIMIT = 56 * 1024 * 1024

SCAN_CHUNK = 128
SCAN_LEVELS = 7
W_PAD = 4
W_PITCH = N_KEYS + W_PAD
W_LHS_PAD = 16
W_GROUP = 32
W_ROWS = 8
FUSE_LANES = 256

F32 = jnp.float32
BF16 = jnp.bfloat16

_NT = (((1,), (1,)), ((), ()))


def _params(*sem):
    return pltpu.CompilerParams(dimension_semantics=sem, vmem_limit_bytes=VMEM_LIMIT)


def _dot(a, b):
    return jnp.dot(a, b, preferred_element_type=F32)


def _dot_nt(a, b):
    return lax.dot_general(a, b, _NT, preferred_element_type=F32)


def _layer_norm(y, g, b):
    mu = jnp.mean(y, axis=-1, keepdims=True)
    yc = y - mu
    var = jnp.mean(yc * yc, axis=-1, keepdims=True)
    return yc * lax.rsqrt(var + LN_EPS) * g + b


def _silu(t):
    return t * (1.0 / (1.0 + jnp.exp(-t)))


def _full(shape):
    nd = len(shape)
    return pl.BlockSpec(shape, lambda *_: (0,) * nd)


def _lower_bounds_kernel(lb_ref, out_ref):
    n = lb_ref.shape[1]
    for d in range(lb_ref.shape[0]):
        rows = [lb_ref[d, j] for j in range(n)]
        m = rows[0]
        for r in rows[1:]:
            m = jnp.maximum(m, r)
        ex = [jnp.exp(r - m) for r in rows]
        tot = ex[0]
        for e in ex[1:]:
            tot = tot + e
        sm = [e / tot for e in ex]
        cum = sm[0]
        out_ref[d, 0] = cum - sm[0]
        for j in range(1, n):
            cum = cum + sm[j]
            out_ref[d, j] = cum - sm[0]


def lower_bounds(hgrn_lb):
    two, n, hk = hgrn_lb.shape
    lb4 = hgrn_lb.astype(F32).reshape(two, n, 1, hk)
    out = pl.pallas_call(
        _lower_bounds_kernel,
        out_shape=jax.ShapeDtypeStruct((two, n, 1, hk), F32),
        name="hgrn_lower_bounds",
    )(lb4)
    return out


def _mla_proj_kernel(x_ref, wa_ref, qn_ref, kvn_ref, wuq_ref, wuqr_ref, wuk_ref, wuv_ref,
                     cq_ref, sq_ref, ck_ref, sk_ref, q_ref, k_ref, v_ref):
    xb = x_ref[...].astype(BF16)
    a = _dot(xb, wa_ref[...])
    c_q = a[:, :MLA_Q_LORA]
    c_kv = a[:, MLA_Q_LORA:MLA_Q_LORA + MLA_KV_LORA]
    kr = a[:, 640:768]
    krr = a[:, 768:896]
    cqn = (c_q * lax.rsqrt(jnp.mean(c_q * c_q, axis=-1, keepdims=True) + RMS_EPS) * qn_ref[...]).astype(BF16)
    ckvn = (c_kv * lax.rsqrt(jnp.mean(c_kv * c_kv, axis=-1, keepdims=True) + RMS_EPS) * kvn_ref[...]).astype(BF16)
    q = _dot(cqn, wuq_ref[...])
    qr = _dot(cqn, wuqr_ref[...])
    cq = cq_ref[...]
    sq = sq_ref[...]
    kro = (kr * ck_ref[...] + krr * sk_ref[...]).astype(BF16)
    kn = _dot(ckvn, wuk_ref[...]).astype(BF16)
    v_ref[...] = _dot(ckvn, wuv_ref[...]).astype(BF16)
    for h in range(MLA_HEADS):
        lo = h * MLA_QK_PAD
        qh = (q[:, lo:lo + MLA_QK_PAD] * cq + qr[:, lo:lo + MLA_QK_PAD] * sq) * (MLA_SCALE * LOG2E)
        q_ref[:, lo:lo + MLA_QK_PAD] = qh.astype(BF16)
        k_ref[:, lo:lo + MLA_NOPE] = kn[:, h * MLA_NOPE:(h + 1) * MLA_NOPE]
        k_ref[:, lo + MLA_NOPE:lo + MLA_QK_PAD] = kro


def mla_proj(x2d, seq, w, tabs, tm=256):
    T = x2d.shape[0]
    tm = math.gcd(seq, tm)
    nper = seq // tm
    cq, sq, ck, sk = tabs
    row = lambda i: (i, 0)
    pos = lambda i: (i % nper, 0)
    hq = MLA_HEADS * MLA_QK_PAD
    return pl.pallas_call(
        _mla_proj_kernel,
        grid=(T // tm,),
        in_specs=[
            pl.BlockSpec((tm, D_MODEL), row),
            _full(w["wa"].shape), _full(w["qn"].shape), _full(w["kvn"].shape),
            _full(w["wuq"].shape), _full(w["wuqr"].shape), _full(w["wuk"].shape), _full(w["wuv"].shape),
            pl.BlockSpec((tm, MLA_QK_PAD), pos), pl.BlockSpec((tm, MLA_QK_PAD), pos),
            pl.BlockSpec((tm, LANES), pos), pl.BlockSpec((tm, LANES), pos),
        ],
        out_specs=[pl.BlockSpec((tm, hq), row), pl.BlockSpec((tm, hq), row),
                   pl.BlockSpec((tm, MLA_HEADS * MLA_V), row)],
        out_shape=[jax.ShapeDtypeStruct((T, hq), BF16), jax.ShapeDtypeStruct((T, hq), BF16),
                   jax.ShapeDtypeStruct((T, MLA_HEADS * MLA_V), BF16)],
        compiler_params=_params("parallel"),
        name="mla_proj",
    )(x2d, w["wa"], w["qn"], w["kvn"], w["wuq"], w["wuqr"], w["wuk"], w["wuv"], cq, sq, ck, sk)


def _attn_kernel(q_ref, kt_ref, v_ref, o_ref, *, tk, unroll):
    q = q_ref[...]
    tq = q.shape[0]
    nk = kt_ref.shape[1] // tk

    def body(c, carry):
        m, l, acc = carry
        for u in range(unroll):
            off = pl.multiple_of((c * unroll + u) * tk, tk)
            s = _dot(q, kt_ref[:, pl.ds(off, tk)])
            m_new = jnp.maximum(m, jnp.max(s, axis=-1, keepdims=True))
            corr = jnp.exp2(m - m_new)
            p = jnp.exp2(s - m_new)
            l = corr * l + jnp.sum(p, axis=-1, keepdims=True)
            acc = corr * acc + _dot(p.astype(BF16), v_ref[pl.ds(off, tk), :])
            m = m_new
        return m, l, acc

    init = (jnp.full((tq, 1), -jnp.inf, F32), jnp.zeros((tq, 1), F32), jnp.zeros((tq, MLA_V), F32))
    _, l, acc = lax.fori_loop(0, nk // unroll, body, init)
    o_ref[...] = (acc / l).astype(o_ref.dtype)


def attention(q2d, kt2d, v2d, batch, seq, tq=512, tk=512, unroll=4):
    T = q2d.shape[0]
    tq, tk = math.gcd(seq, tq), math.gcd(seq, tk)
    unroll = math.gcd(seq // tk, unroll)
    nq = seq // tq
    return pl.pallas_call(
        functools.partial(_attn_kernel, tk=tk, unroll=unroll),
        grid=(batch, MLA_HEADS, nq),
        in_specs=[
            pl.BlockSpec((tq, MLA_QK_PAD), lambda b, h, i: (b * nq + i, h)),
            pl.BlockSpec((MLA_QK_PAD, seq), lambda b, h, i: (h, b)),
            pl.BlockSpec((seq, MLA_V), lambda b, h, i: (b, h)),
        ],
        out_specs=pl.BlockSpec((tq, MLA_V), lambda b, h, i: (b * nq + i, h)),
        out_shape=jax.ShapeDtypeStruct((T, MLA_HEADS * MLA_V), BF16),
        compiler_params=_params("parallel", "parallel", "arbitrary"),
        name="mla_attention",
    )(q2d, kt2d, v2d)


def _proj_res_ln_kernel(a_ref, x_ref, w_ref, g_ref, b_ref, o_ref, ob_ref, *, alpha):
    h = _dot(a_ref[...], w_ref[...])
    y = _layer_norm(alpha * x_ref[...] + h, g_ref[...], b_ref[...])
    o_ref[...] = y
    ob_ref[...] = y.astype(BF16)


def proj_res_ln(a2d, x2d, w, g, b, alpha, tm=512):
    T = x2d.shape[0]
    tm = math.gcd(T, tm)
    row = lambda i: (i, 0)
    return pl.pallas_call(
        functools.partial(_proj_res_ln_kernel, alpha=alpha),
        grid=(T // tm,),
        in_specs=[pl.BlockSpec((tm, a2d.shape[1]), row), pl.BlockSpec((tm, D_MODEL), row),
                  _full(w.shape), _full(g.shape), _full(b.shape)],
        out_specs=[pl.BlockSpec((tm, D_MODEL), row), pl.BlockSpec((tm, D_MODEL), row)],
        out_shape=[jax.ShapeDtypeStruct((T, D_MODEL), F32), jax.ShapeDtypeStruct((T, D_MODEL), BF16)],
        compiler_params=_params("parallel"),
        name="proj_res_ln",
    )(a2d, x2d, w, g, b)


def _matmul_kernel(x_ref, w_ref, o_ref):
    o_ref[...] = _dot(x_ref[...].astype(BF16), w_ref[...])


def matmul(x2d, w, tm=512, tn=1280):
    T, K = x2d.shape
    N = w.shape[1]
    tm = math.gcd(T, tm)
    return pl.pallas_call(
        _matmul_kernel,
        grid=(T // tm, N // tn),
        in_specs=[pl.BlockSpec((tm, K), lambda i, j: (i, 0)), pl.BlockSpec((K, tn), lambda i, j: (0, j))],
        out_specs=pl.BlockSpec((tm, tn), lambda i, j: (i, j)),
        out_shape=jax.ShapeDtypeStruct((T, N), F32),
        compiler_params=_params("parallel", "arbitrary"),
        name="matmul",
    )(x2d, w)


def _scan_tables():
    C, L = SCAN_CHUNK, SCAN_LEVELS
    mats = np.zeros((2, (2 + L) * C, C), np.float32)
    lvl = np.full((2, C, C), -1, np.int32)
    for d in range(2):
        p = np.arange(C) if d == 0 else C - 1 - np.arange(C)
        pt, pu = p[:, None], p[None, :]
        mats[d, 0:C] = pu <= pt
        mats[d, C:2 * C] = pu > pt
        for l in range(L):
            m = 1 << l
            r = (pt // (2 * m)) * (2 * m) + m - 1
            qside = (pt % (2 * m)) >= m
            e = np.where(qside, (pu > r) & (pu <= pt), (pu > pt) & (pu <= r))
            mats[d, (2 + l) * C:(3 + l) * C] = e
            same = (pt // (2 * m)) == (pu // (2 * m))
            lvl[d][same & qside & ((pu % (2 * m)) < m)] = l
        lvl[d][pt == pu] = L
    return jnp.asarray(np.concatenate([mats, mats], axis=2), BF16), jnp.asarray(np.concatenate([lvl, lvl], axis=2))


def _hgrn_scan_kernel(q_ref, z_ref, v_ref, lb_ref, mat_ref, lvl_ref, o_ref, state_ref, kk_ref, ex_ref):
    C, L = SCAN_CHUNK, SCAN_LEVELS

    @pl.when(pl.program_id(2) == 0)
    def _():
        state_ref[...] = jnp.zeros_like(state_ref)

    z = z_ref[...]
    lb = lb_ref[...]
    e = jnp.exp(-jnp.abs(z))
    r = 1.0 / (1.0 + e)
    er = e * r
    pos = z >= 0
    f = lb + (1.0 - lb) * jnp.where(pos, r, er)
    g = jnp.log(jnp.maximum(f, GATE_FLOOR))
    kk_ref[...] = (1.0 - lb) * jnp.where(pos, er, r)
    g1 = g.astype(BF16)
    g2 = (g - g1.astype(F32)).astype(BF16)
    ex_ref[...] = _dot(mat_ref[...], jnp.concatenate([g1, g2], axis=0))

    lvl = lvl_ref[...]
    zero = jnp.zeros((C, HGRN_DK), BF16)
    wide = 2 * HGRN_DK

    def block_diag(x):
        return jnp.concatenate([jnp.concatenate([x[:, :HGRN_DK], zero], axis=1),
                                jnp.concatenate([zero, x[:, HGRN_DK:]], axis=1)], axis=0)

    for hp in range(HGRN_HEADS // 2):
        sl = slice(hp * wide, (hp + 1) * wide)
        q = _silu(q_ref[:, sl])
        kk = kk_ref[:, sl]
        qb = q.astype(BF16)
        kb = kk.astype(BF16)
        attn = jnp.where(lvl == L, _dot_nt(qb, block_diag(kb)), 0.0)
        for l in range(L):
            xl = jnp.exp(ex_ref[(2 + l) * C:(3 + l) * C, sl]).astype(BF16)
            p = _dot_nt(qb * xl, block_diag(kb * xl))
            attn = jnp.where(lvl == l, p, attn)
        v2 = v_ref[:, sl]
        e_b = ex_ref[0:C, sl]
        e_last = ex_ref[C:2 * C, sl]
        b_last = e_b[0:1] + e_last[0:1]
        st = [state_ref[2 * hp + d] for d in range(2)]
        st_bd = jnp.concatenate([jnp.concatenate([st[0].astype(BF16), zero], axis=1),
                                 jnp.concatenate([zero, st[1].astype(BF16)], axis=1)], axis=0)
        o = _dot(attn.astype(BF16), block_diag(v2.astype(BF16)))
        o_ref[:, sl] = o + _dot_nt((q * jnp.exp(e_b)).astype(BF16), st_bd)
        kd = (kk * jnp.exp(e_last)).astype(BF16)
        for d in range(2):
            hs = slice(d * HGRN_DK, (d + 1) * HGRN_DK)
            vt = v2[:, hs].T.astype(BF16)
            state_ref[2 * hp + d] = st[d] * jnp.exp(b_last[:, hs]) + _dot(vt, kd[:, hs])


def hgrn_scan(proj, lb, batch, seq):
    T = proj.shape[0]
    C = SCAN_CHUNK
    nc = seq // C
    hk = HGRN_HEADS * HGRN_DK
    mats, lvl = _scan_tables()

    def rowblk(b, d, c):
        return b * nc + c + d * (nc - 1 - 2 * c)

    return pl.pallas_call(
        _hgrn_scan_kernel,
        grid=(batch, 2, nc),
        in_specs=[
            pl.BlockSpec((C, hk), lambda b, d, c: (rowblk(b, d, c), 0)),
            pl.BlockSpec((C, hk), lambda b, d, c: (rowblk(b, d, c), 1 + d)),
            pl.BlockSpec((C, hk), lambda b, d, c: (rowblk(b, d, c), 3)),
            pl.BlockSpec((None, 1, hk), lambda b, d, c: (d, 0, 0)),
            pl.BlockSpec((None,) + mats.shape[1:], lambda b, d, c: (d, 0, 0)),
            pl.BlockSpec((None, C, 2 * C), lambda b, d, c: (d, 0, 0)),
        ],
        out_specs=pl.BlockSpec((None, C, hk), lambda b, d, c: (d, rowblk(b, d, c), 0)),
        out_shape=jax.ShapeDtypeStruct((2, T, hk), F32),
        scratch_shapes=[pltpu.VMEM((HGRN_HEADS, HGRN_DV, HGRN_DK), F32), pltpu.VMEM((C, hk), F32),
                        pltpu.VMEM(((2 + SCAN_LEVELS) * C, hk), F32)],
        compiler_params=_params("parallel", "arbitrary", "arbitrary"),
        name="hgrn_scan",
    )(proj, proj, proj, lb, mats, lvl)


def _hgrn_out_kernel(o_ref, g_ref, x_ref, nw_ref, w_ref, lg_ref, lbias_ref, y_ref, yb_ref, a_ref, *, alpha):
    nw = nw_ref[...]
    for h in range(HGRN_HEADS):
        sl = slice(h * HGRN_DV, (h + 1) * HGRN_DV)
        o = o_ref[0, :, sl] + o_ref[1, :, sl]
        on = o * lax.rsqrt(jnp.mean(o * o, axis=-1, keepdims=True) + RMS_EPS) * nw
        a_ref[:, sl] = (on * _silu(g_ref[:, sl])).astype(BF16)
    hmix = _dot(a_ref[...], w_ref[...])
    y = _layer_norm(alpha * x_ref[...] + hmix, lg_ref[...], lbias_ref[...])
    y_ref[...] = y
    yb_ref[...] = y.astype(BF16)


def hgrn_out(o2, proj, x2d, nw, w, g, b, alpha, tm=256):
    T = x2d.shape[0]
    tm = math.gcd(T, tm)
    hv = HGRN_HEADS * HGRN_DV
    row = lambda i: (i, 0)
    return pl.pallas_call(
        functools.partial(_hgrn_out_kernel, alpha=alpha),
        grid=(T // tm,),
        in_specs=[pl.BlockSpec((2, tm, hv), lambda i: (0, i, 0)),
                  pl.BlockSpec((tm, hv), lambda i: (i, 4)),
                  pl.BlockSpec((tm, D_MODEL), row),
                  _full(nw.shape), _full(w.shape), _full(g.shape), _full(b.shape)],
        out_specs=[pl.BlockSpec((tm, D_MODEL), row), pl.BlockSpec((tm, D_MODEL), row)],
        out_shape=[jax.ShapeDtypeStruct((T, D_MODEL), F32), jax.ShapeDtypeStruct((T, D_MODEL), BF16)],
        scratch_shapes=[pltpu.VMEM((tm, hv), BF16)],
        compiler_params=_params("parallel"),
        name="hgrn_out",
    )(o2, proj, x2d, nw, w, g, b)


_PAIRS = [(a, b) for a in range(PEER_TOPK) for b in range(PEER_TOPK) if (a + 1) * (b + 1) <= PEER_TOPK]
_NPAIR_PAD = -(-len(_PAIRS) // 8) * 8


def _top_rows(s, k, payload=None):
    n = s.shape[0]
    iota = lax.broadcasted_iota(jnp.int32, s.shape, 0)
    sub = 8
    iota8 = lax.broadcasted_iota(jnp.int32, (sub, s.shape[1]), 0)
    vals, idxs = [], []
    for _ in range(k):
        nodes = [(s[r:r + sub], iota8 + r) for r in range(0, n, sub)]
        while len(nodes) > 1:
            nxt = [(jnp.maximum(va, vb), jnp.where(va >= vb, ia, ib))
                   for (va, ia), (vb, ib) in zip(nodes[0::2], nodes[1::2])]
            nodes = nxt + ([nodes[-1]] if len(nodes) % 2 else [])
        v8, i8 = nodes[0]
        m = jnp.max(v8, axis=0, keepdims=True)
        im = jnp.min(jnp.where(v8 == m, i8, n), axis=0, keepdims=True)
        hit = iota == im
        vals.append(m)
        if payload is None:
            idxs.append(im)
        else:
            idxs.append(jnp.max(jnp.where(hit, payload, -1), axis=0, keepdims=True))
        s = jnp.where(hit, -jnp.inf, s)
    return vals, idxs


def _retrieve_head(q_halves, keys_ref):
    sv, si = [], []
    for c in range(2):
        s = _dot_nt(keys_ref[c], q_halves[c])
        v_, i_ = _top_rows(s, PEER_TOPK)
        sv.append(v_)
        si.append(i_)
    return _combine_halves(sv, si)


def _combine_halves(sv, si):
    tm = sv[0][0].shape[1]
    neg = jnp.full((_NPAIR_PAD - len(_PAIRS), tm), -jnp.inf, F32)
    zero = jnp.zeros((_NPAIR_PAD - len(_PAIRS), tm), jnp.int32)
    cand = jnp.concatenate([sv[0][a] + sv[1][b] for a, b in _PAIRS] + [neg], axis=0)
    cid = jnp.concatenate([si[0][a] * N_KEYS + si[1][b] for a, b in _PAIRS] + [zero], axis=0)
    tv, te = _top_rows(cand, PEER_TOPK, payload=cid)
    tv = jnp.concatenate(tv, axis=0)
    ex = jnp.exp(tv - tv[0:1])
    return ex / jnp.sum(ex, axis=0, keepdims=True), jnp.concatenate(te, axis=0)


def _peer_topk_kernel(xb_ref, wq_ref, keys_ref, gate_ref, eid_ref, q_ref):
    q_ref[...] = _dot(xb_ref[...], wq_ref[...]).astype(BF16)
    half = PEER_DK // 2

    def head(h, carry):
        off = pl.multiple_of(h * PEER_DK, PEER_DK)
        gate, eid = _retrieve_head([q_ref[:, pl.ds(off + c * half, half)] for c in range(2)], keys_ref)
        row = pl.multiple_of(h * PEER_TOPK, PEER_TOPK)
        gate_ref[pl.ds(row, PEER_TOPK), :] = gate
        eid_ref[pl.ds(row, PEER_TOPK), :] = eid
        return carry

    lax.fori_loop(0, PEER_HEADS, head, 0)


def peer_topk(xb, wq, keys, tm=256):
    T = xb.shape[0]
    tm = math.gcd(T, tm)
    ns = PEER_HEADS * PEER_TOPK
    return pl.pallas_call(
        _peer_topk_kernel,
        grid=(T // tm,),
        in_specs=[pl.BlockSpec((tm, D_MODEL), lambda i: (i, 0)), _full(wq.shape), _full(keys.shape)],
        out_specs=[pl.BlockSpec((ns, tm), lambda i: (0, i)), pl.BlockSpec((ns, tm), lambda i: (0, i))],
        out_shape=[jax.ShapeDtypeStruct((ns, T), F32), jax.ShapeDtypeStruct((ns, T), jnp.int32)],
        scratch_shapes=[pltpu.VMEM((tm, PEER_HEADS * PEER_DK), BF16)],
        compiler_params=_params("parallel"),
        name="peer_topk",
    )(xb, wq, keys)


def _peer_w_kernel(gate_ref, eid_ref, w_ref, gt_ref, it_ref, jt_ref, tile_ref):
    tm = gate_ref.shape[1]
    eid_t = eid_ref[...].T
    gt_ref[...] = gate_ref[...].T
    it_ref[...] = lax.shift_right_logical(eid_t, int(math.log2(N_KEYS)))
    jt_ref[...] = lax.bitwise_and(eid_t, N_KEYS - 1)
    ns = gate_ref.shape[0]
    iota = lax.broadcasted_iota(jnp.int32, (N_KEYS, ns), 0)
    iota_pad = lax.broadcasted_iota(jnp.int32, (N_KEYS + W_LHS_PAD, ns), 0)
    zero = jnp.zeros((N_KEYS, ns), BF16)

    def token_group(p, carry):
        base = pl.multiple_of(p * W_GROUP, W_GROUP)
        g8 = gt_ref[pl.ds(base, W_GROUP), :]
        i8 = it_ref[pl.ds(base, W_GROUP), :]
        j8 = jt_ref[pl.ds(base, W_GROUP), :]
        for u in range(0, W_GROUP, 2):
            at0 = jnp.where(iota_pad == i8[u:u + 1], g8[u:u + 1], 0.0).astype(BF16)
            at1 = jnp.where(iota_pad == i8[u + 1:u + 2] + W_PAD, g8[u + 1:u + 2], 0.0).astype(BF16)
            bt = [jnp.where(iota == j8[u + d:u + d + 1], 1.0, 0.0).astype(BF16) for d in range(2)]
            lhs = jnp.concatenate([at0, at1], axis=1)
            rhs = jnp.concatenate([jnp.concatenate([bt[0], zero], axis=1),
                                   jnp.concatenate([zero, bt[1]], axis=1)], axis=0)
            res = _dot_nt(lhs, rhs)
            row = pl.multiple_of((base + u) * W_PITCH, 8)
            tile_ref[pl.ds(row, N_KEYS), :] = res[:N_KEYS, :N_KEYS]
            tile_ref[pl.ds(row + N_KEYS, N_KEYS + 2 * W_PAD), :] = res[:N_KEYS + 2 * W_PAD, N_KEYS:]
        return carry

    lax.fori_loop(0, tm // W_GROUP, token_group, 0)

    def key_rows(i2, carry):
        for d in range(W_ROWS):
            i = i2 * W_ROWS + d
            col = tile_ref[pl.ds(i, tm, stride=W_PITCH), :]
            w_ref[:, pl.ds(pl.multiple_of(i * N_KEYS, N_KEYS), N_KEYS)] = col.astype(BF16)
        return carry

    lax.fori_loop(0, N_KEYS // W_ROWS, key_rows, 0)


def peer_w(gate, eid, tm=128):
    ns, T = gate.shape
    tm = math.gcd(T, tm)
    return pl.pallas_call(
        _peer_w_kernel,
        grid=(T // tm,),
        in_specs=[pl.BlockSpec((ns, tm), lambda i: (0, i)), pl.BlockSpec((ns, tm), lambda i: (0, i))],
        out_specs=pl.BlockSpec((tm, N_EXPERTS), lambda i: (i, 0)),
        out_shape=jax.ShapeDtypeStruct((T, N_EXPERTS), BF16),
        scratch_shapes=[pltpu.VMEM((tm, ns), F32), pltpu.VMEM((tm, ns), jnp.int32),
                        pltpu.VMEM((tm, ns), jnp.int32), pltpu.VMEM((tm * W_PITCH, N_KEYS), F32)],
        compiler_params=_params("parallel"),
        name="peer_gate_matrix",
    )(gate, eid)


def _dense_init(o_ref):
    @pl.when(pl.program_id(1) == 0)
    def _():
        o_ref[...] = jnp.zeros_like(o_ref)


def _dense_step(xb_ref, ut_ref, v_ref, w_ref, o_ref):
    h = _dot(xb_ref[...], ut_ref[...])
    act = 0.5 * h * (1.0 + lax.erf(h * (1.0 / math.sqrt(2.0))))
    hw = act * w_ref[...].astype(F32)
    o_ref[...] += _dot(hw.astype(BF16), v_ref[...])
    return hw


def _after(x, marker):
    z = pltpu.bitcast(marker, jnp.uint32)
    z = lax.shift_right_logical(lax.shift_right_logical(z, jnp.uint32(31)), jnp.uint32(1))
    z = pltpu.bitcast(z, F32)
    return x + jnp.tile(z, (x.shape[0] // z.shape[0], x.shape[1] // z.shape[1]))


def _peer_dense_kernel(xb_ref, ut_ref, v_ref, w_ref, o_ref):
    _dense_init(o_ref)
    _dense_step(xb_ref, ut_ref, v_ref, w_ref, o_ref)


def _peer_dense_topk_kernel(xb_ref, ut_ref, v_ref, w_ref, xn_ref, wq_ref, keys_ref, o_ref, gate_ref, eid_ref):
    _dense_init(o_ref)
    half = PEER_DK // 2
    q = _dot(xn_ref[...], wq_ref[...]).astype(BF16)
    scores = [_dot_nt(keys_ref[c], q[:, c * half:(c + 1) * half]) for c in range(2)]
    hw = _dense_step(xb_ref, ut_ref, v_ref, w_ref, o_ref)
    marker = hw[hw.shape[0] - 8:, :LANES]
    row = pl.multiple_of(pl.program_id(1) * PEER_TOPK, PEER_TOPK)
    tm = xn_ref.shape[0]
    sub = math.gcd(tm, FUSE_LANES)
    for t0 in range(0, tm, sub):
        top = [_top_rows(_after(scores[c][:, t0:t0 + sub], marker), PEER_TOPK) for c in range(2)]
        gate, eid = _combine_halves([top[0][0], top[1][0]], [top[0][1], top[1][1]])
        gate_ref[pl.ds(row, PEER_TOPK), t0:t0 + sub] = gate
        eid_ref[pl.ds(row, PEER_TOPK), t0:t0 + sub] = eid
        marker = gate[:8]


def _dense_specs(tm, ne):
    return [pl.BlockSpec((tm, D_MODEL), lambda i, j: (i, 0)),
            pl.BlockSpec((D_MODEL, ne), lambda i, j: (0, j)),
            pl.BlockSpec((ne, D_MODEL), lambda i, j: (j, 0)),
            pl.BlockSpec((tm, ne), lambda i, j: (i, j))]


def peer_dense(xb, ut, v, wmat, tm=512, ne=N_EXPERTS // PEER_HEADS):
    T = xb.shape[0]
    tm = math.gcd(T, tm)
    return pl.pallas_call(
        _peer_dense_kernel,
        grid=(T // tm, N_EXPERTS // ne),
        in_specs=_dense_specs(tm, ne),
        out_specs=pl.BlockSpec((tm, D_MODEL), lambda i, j: (i, 0)),
        out_shape=jax.ShapeDtypeStruct((T, D_MODEL), F32),
        compiler_params=_params("parallel", "arbitrary"),
        name="peer_dense",
    )(xb, ut, v, wmat)


def peer_dense_topk(xb, ut, v, wmat, xn, wq, keys, tm=512):
    T = xb.shape[0]
    assert xn.shape[0] == T
    tm = math.gcd(T, tm)
    ne = N_EXPERTS // PEER_HEADS
    ns = PEER_HEADS * PEER_TOPK
    return pl.pallas_call(
        _peer_dense_topk_kernel,
        grid=(T // tm, PEER_HEADS),
        in_specs=_dense_specs(tm, ne) + [
            pl.BlockSpec((tm, D_MODEL), lambda i, j: (i, 0)),
            pl.BlockSpec((D_MODEL, PEER_DK), lambda i, j: (0, j)),
            _full(keys.shape)],
        out_specs=[pl.BlockSpec((tm, D_MODEL), lambda i, j: (i, 0)),
                   pl.BlockSpec((ns, tm), lambda i, j: (0, i)), pl.BlockSpec((ns, tm), lambda i, j: (0, i))],
        out_shape=[jax.ShapeDtypeStruct((T, D_MODEL), F32),
                   jax.ShapeDtypeStruct((ns, T), F32), jax.ShapeDtypeStruct((ns, T), jnp.int32)],
        compiler_params=_params("parallel", "arbitrary"),
        name="peer_dense_topk",
    )(xb, ut, v, wmat, xn, wq, keys)


def _ln_ple_kernel(x_ref, f_ref, p_ref, g_ref, b_ref, wg_ref, bg_ref, wp_ref, o_ref, *, alpha):
    y = _layer_norm(alpha * x_ref[...] + f_ref[...], g_ref[...], b_ref[...])
    gate = 1.0 / (1.0 + jnp.exp(-(_dot(y.astype(BF16), wg_ref[...]) + bg_ref[...])))
    o_ref[...] = y + gate * _dot(p_ref[...].astype(BF16), wp_ref[...])


def ln_ple(x2d, f2d, p2d, p_row0, g, b, wg, bg, wp, alpha, tm=512):
    T = x2d.shape[0]
    tm = math.gcd(math.gcd(T, tm), p_row0) if p_row0 else math.gcd(T, tm)
    blk0 = p_row0 // tm
    row = lambda i: (i, 0)
    return pl.pallas_call(
        functools.partial(_ln_ple_kernel, alpha=alpha),
        grid=(T // tm,),
        in_specs=[pl.BlockSpec((tm, D_MODEL), row), pl.BlockSpec((tm, D_MODEL), row),
                  pl.BlockSpec((tm, PLE_DIM), lambda i: (i + blk0, 0)),
                  _full(g.shape), _full(b.shape), _full(wg.shape), _full(bg.shape), _full(wp.shape)],
        out_specs=pl.BlockSpec((tm, D_MODEL), row),
        out_shape=jax.ShapeDtypeStruct((T, D_MODEL), F32),
        compiler_params=_params("parallel"),
        name="ln_ple",
    )(x2d, f2d, p2d, g, b, wg, bg, wp)


def _rot_cols(w):
    half = MLA_ROPE // 2
    return jnp.concatenate([-w[..., half:], w[..., :half]], axis=-1)


def _mla_weights(w_a, q_norm, kv_norm, w_uq, w_ukv):
    zk = jnp.zeros((D_MODEL, LANES - MLA_ROPE), F32)
    w_kr = w_a[:, MLA_Q_LORA + MLA_KV_LORA:]
    wa = jnp.concatenate([w_a[:, :MLA_Q_LORA + MLA_KV_LORA], w_kr, zk, _rot_cols(w_kr), zk], axis=1)
    uq = w_uq.reshape(MLA_Q_LORA, MLA_HEADS, MLA_NOPE + MLA_ROPE)
    zq = jnp.zeros((MLA_Q_LORA, MLA_HEADS, MLA_QK_PAD - MLA_NOPE - MLA_ROPE), F32)
    wuq = jnp.concatenate([uq, zq], axis=-1).reshape(MLA_Q_LORA, MLA_HEADS * MLA_QK_PAD)
    wuqr = jnp.concatenate([jnp.zeros_like(uq[..., :MLA_NOPE]), _rot_cols(uq[..., MLA_NOPE:]), zq],
                           axis=-1).reshape(MLA_Q_LORA, MLA_HEADS * MLA_QK_PAD)
    ukv = w_ukv.reshape(MLA_KV_LORA, MLA_HEADS, MLA_NOPE + MLA_V)
    return {
        "wa": wa.astype(BF16),
        "qn": q_norm.astype(F32).reshape(1, -1),
        "kvn": kv_norm.astype(F32).reshape(1, -1),
        "wuq": wuq.astype(BF16),
        "wuqr": wuqr.astype(BF16),
        "wuk": ukv[..., :MLA_NOPE].reshape(MLA_KV_LORA, -1).astype(BF16),
        "wuv": ukv[..., MLA_NOPE:].reshape(MLA_KV_LORA, -1).astype(BF16),
    }


def _rope_tabs(seq):
    inv_freq = 1.0 / (ROPE_THETA ** (jnp.arange(0, MLA_ROPE, 2, dtype=F32) / MLA_ROPE))
    ang = jnp.arange(seq, dtype=F32)[:, None] * inv_freq[None, :]
    emb = jnp.concatenate([ang, ang], axis=-1)
    cos, sin = jnp.cos(emb), jnp.sin(emb)
    one = jnp.ones((seq, MLA_NOPE), F32)
    z64 = jnp.zeros((seq, LANES - MLA_ROPE), F32)
    z128 = jnp.zeros((seq, MLA_NOPE), F32)
    return (jnp.concatenate([one, cos, z64], axis=1), jnp.concatenate([z128, sin, z64], axis=1),
            jnp.concatenate([cos, z64], axis=1), jnp.concatenate([sin, z64], axis=1))


def _mixer(i, x2d, batch, seq, lbs, mla_w, hgrn_w, ln_w, alpha, tabs):
    j = i // 2
    g1, b1, _, _ = ln_w[i]
    if i % 2 == 0:
        w = mla_w[j]
        q2d, k2d, v2d = mla_proj(x2d, seq, w, tabs)
        o2d = attention(q2d, k2d.T, v2d, batch, seq)
        return proj_res_ln(o2d, x2d, w["wo"], g1, b1, alpha)
    w = hgrn_w[j]
    proj = matmul(x2d, w["win"])
    o2 = hgrn_scan(proj, lbs[:, j], batch, seq)
    return hgrn_out(o2, proj, x2d, w["nw"], w["wo"], g1, b1, alpha)


def _peer_groups(x1b, pw):
    n = len(x1b)
    if n > 1 and all(x.shape[0] == x1b[0].shape[0] for x in x1b):
        out = []
        sel = peer_topk(x1b[0], pw["wq"], pw["keys"])
        for k in range(n):
            wmat = peer_w(*sel)
            if k + 1 < n:
                f, gate, eid = peer_dense_topk(x1b[k], pw["ut"], pw["v"], wmat, x1b[k + 1], pw["wq"], pw["keys"])
                sel = (gate, eid)
            else:
                f = peer_dense(x1b[k], pw["ut"], pw["v"], wmat)
            out.append(f)
        return out
    return [peer_dense(x, pw["ut"], pw["v"], peer_w(*peer_topk(x, pw["wq"], pw["keys"]))) for x in x1b]


def _trunk(groups, lbs, mla_w, hgrn_w, peer_w_, ln_w, ple_w, depth):
    alpha = (2 * depth) ** 0.25
    tabs = {g["seq"]: None for g in groups}
    for s in tabs:
        tabs[s] = _rope_tabs(s)
    xs = [g["x"] for g in groups]
    for i in range(depth):
        mixed = [_mixer(i, x, g["batch"], g["seq"], lbs, mla_w, hgrn_w, ln_w, alpha, tabs[g["seq"]])
                 for x, g in zip(xs, groups)]
        fs = _peer_groups([m[1] for m in mixed], peer_w_[i])
        _, _, g2, b2 = ln_w[i]
        wg, bg, wp = ple_w[i]
        xs = [ln_ple(m[0], f, g["p"][i], g["row0"], g2, b2, wg, bg, wp, alpha)
              for m, f, g in zip(mixed, fs, groups)]
    return xs


def _token_groups(x, p, target):
    batch, seq, _ = x.shape
    depth = p.shape[0]
    T = batch * seq
    x2d = x.reshape(T, D_MODEL)
    p2d = p.reshape(depth, T, PLE_DIM)
    n = T // target if (target % seq == 0 and T % target == 0) else 1
    tg = T // n
    return [{"x": x2d if n == 1 else x2d[k * tg:(k + 1) * tg], "p": p2d, "row0": k * tg,
             "batch": batch // n, "seq": seq} for k in range(n)]


def kernel(x_prompt, x_sample, p_prompt, p_sample, mla_w_a, mla_q_norm, mla_kv_norm, mla_w_uq, mla_w_ukv, mla_w_o, hgrn_w_in, hgrn_lb, hgrn_norm, hgrn_w_o, peer_w_q, peer_sub_keys, peer_u, peer_v, ln1_g, ln1_b, ln2_g, ln2_b, ple_gate_w, ple_gate_b, ple_proj):
    depth = peer_w_q.shape[0]
    lbs = lower_bounds(hgrn_lb)
    mla_w = []
    for j in range(mla_w_a.shape[0]):
        w = _mla_weights(mla_w_a[j], mla_q_norm[j], mla_kv_norm[j], mla_w_uq[j], mla_w_ukv[j])
        w["wo"] = mla_w_o[j].astype(BF16)
        mla_w.append(w)
    hgrn_w = [{"win": hgrn_w_in[j].astype(BF16), "nw": hgrn_norm[j].astype(F32).reshape(1, -1),
               "wo": hgrn_w_o[j].astype(BF16)} for j in range(hgrn_w_in.shape[0])]
    peer_w_ = [{"wq": peer_w_q[i].astype(BF16), "keys": peer_sub_keys[i].astype(BF16),
                "ut": peer_u[i].astype(BF16).T, "v": peer_v[i].astype(BF16)} for i in range(depth)]
    r = lambda a: a.astype(F32).reshape(1, -1)
    ln_w = [(r(ln1_g[i]), r(ln1_b[i]), r(ln2_g[i]), r(ln2_b[i])) for i in range(depth)]
    ple_w = [(ple_gate_w[i].astype(BF16), r(ple_gate_b[i]), ple_proj[i].astype(BF16)) for i in range(depth)]
    target = x_prompt.shape[0] * x_prompt.shape[1]
    groups = _token_groups(x_prompt, p_prompt, target) + _token_groups(x_sample, p_sample, target)
    ys = _trunk(groups, lbs, mla_w, hgrn_w, peer_w_, ln_w, ple_w, depth)
    y_prompt = ys[0].reshape(x_prompt.shape)
    y_sample = (ys[1] if len(ys) == 2 else jnp.concatenate(ys[1:], axis=0)).reshape(x_sample.shape)
    return (y_prompt, y_sample)
```

```python
import functools
import math

import numpy as np
import jax
import jax.numpy as jnp
from jax import lax
from jax.experimental import pallas as pl
from jax.experimental.pallas import tpu as pltpu

D_MODEL = 1024
PLE_DIM = 256
MLA_HEADS = 8
MLA_NOPE = 128
MLA_ROPE = 64
MLA_V = 128
MLA_Q_LORA = 384
MLA_KV_LORA = 256
MLA_SCALE = (MLA_NOPE + MLA_ROPE) ** -0.5
MLA_QK_PAD = 256
LOG2E = math.log2(math.e)
ROPE_THETA = 10000.0
HGRN_HEADS = 8
HGRN_DK = 128
HGRN_DV = 128
GATE_FLOOR = 1e-30
PEER_HEADS = 8
PEER_DK = 256
N_KEYS = 128
N_EXPERTS = N_KEYS * N_KEYS
PEER_TOPK = 16
LN_EPS = 1e-5
RMS_EPS = 1e-6

LANES = 128
VMEM_LIMIT = 56 * 1024 * 1024

SCAN_CHUNK = 128
SCAN_LEVELS = 7
W_PAD = 4
W_PITCH = N_KEYS + W_PAD
W_LHS_PAD = 16
W_GROUP = 32
W_ROWS = 8
FUSE_LANES = 256

F32 = jnp.float32
BF16 = jnp.bfloat16

_NT = (((1,), (1,)), ((), ()))


def _params(*sem):
    return pltpu.CompilerParams(dimension_semantics=sem, vmem_limit_bytes=VMEM_LIMIT)


def _dot(a, b):
    return jnp.dot(a, b, preferred_element_type=F32)


def _dot_nt(a, b):
    return lax.dot_general(a, b, _NT, preferred_element_type=F32)


def _layer_norm(y, g, b):
    mu = jnp.mean(y, axis=-1, keepdims=True)
    yc = y - mu
    var = jnp.mean(yc * yc, axis=-1, keepdims=True)
    return yc * lax.rsqrt(var + LN_EPS) * g + b


def _silu(t):
    return t * (1.0 / (1.0 + jnp.exp(-t)))


def _full(shape):
    nd = len(shape)
    return pl.BlockSpec(shape, lambda *_: (0,) * nd)


def _lower_bounds_kernel(lb_ref, out_ref):
    n = lb_ref.shape[1]
    for d in range(lb_ref.shape[0]):
        rows = [lb_ref[d, j] for j in range(n)]
        m = rows[0]
        for r in rows[1:]:
            m = jnp.maximum(m, r)
        ex = [jnp.exp(r - m) for r in rows]
        tot = ex[0]
        for e in ex[1:]:
            tot = tot + e
        sm = [e / tot for e in ex]
        cum = sm[0]
        out_ref[d, 0] = cum - sm[0]
        for j in range(1, n):
            cum = cum + sm[j]
            out_ref[d, j] = cum - sm[0]


def lower_bounds(hgrn_lb):
    two, n, hk = hgrn_lb.shape
    lb4 = hgrn_lb.astype(F32).reshape(two, n, 1, hk)
    out = pl.pallas_call(
        _lower_bounds_kernel,
        out_shape=jax.ShapeDtypeStruct((two, n, 1, hk), F32),
        name="hgrn_lower_bounds",
    )(lb4)
    return out


def _mla_proj_kernel(x_ref, wa_ref, qn_ref, kvn_ref, wuq_ref, wuqr_ref, wuk_ref, wuv_ref,
                     cq_ref, sq_ref, ck_ref, sk_ref, q_ref, k_ref, v_ref):
    xb = x_ref[...].astype(BF16)
    a = _dot(xb, wa_ref[...])
    c_q = a[:, :MLA_Q_LORA]
    c_kv = a[:, MLA_Q_LORA:MLA_Q_LORA + MLA_KV_LORA]
    kr = a[:, 640:768]
    krr = a[:, 768:896]
    cqn = (c_q * lax.rsqrt(jnp.mean(c_q * c_q, axis=-1, keepdims=True) + RMS_EPS) * qn_ref[...]).astype(BF16)
    ckvn = (c_kv * lax.rsqrt(jnp.mean(c_kv * c_kv, axis=-1, keepdims=True) + RMS_EPS) * kvn_ref[...]).astype(BF16)
    q = _dot(cqn, wuq_ref[...])
    qr = _dot(cqn, wuqr_ref[...])
    cq = cq_ref[...]
    sq = sq_ref[...]
    kro = (kr * ck_ref[...] + krr * sk_ref[...]).astype(BF16)
    kn = _dot(ckvn, wuk_ref[...]).astype(BF16)
    v_ref[...] = _dot(ckvn, wuv_ref[...]).astype(BF16)
    for h in range(MLA_HEADS):
        lo = h * MLA_QK_PAD
        qh = (q[:, lo:lo + MLA_QK_PAD] * cq + qr[:, lo:lo + MLA_QK_PAD] * sq) * (MLA_SCALE * LOG2E)
        q_ref[:, lo:lo + MLA_QK_PAD] = qh.astype(BF16)
        k_ref[:, lo:lo + MLA_NOPE] = kn[:, h * MLA_NOPE:(h + 1) * MLA_NOPE]
        k_ref[:, lo + MLA_NOPE:lo + MLA_QK_PAD] = kro


def mla_proj(x2d, seq, w, tabs, tm=256):
    T = x2d.shape[0]
    tm = math.gcd(seq, tm)
    nper = seq // tm
    cq, sq, ck, sk = tabs
    row = lambda i: (i, 0)
    pos = lambda i: (i % nper, 0)
    hq = MLA_HEADS * MLA_QK_PAD
    return pl.pallas_call(
        _mla_proj_kernel,
        grid=(T // tm,),
        in_specs=[
            pl.BlockSpec((tm, D_MODEL), row),
            _full(w["wa"].shape), _full(w["qn"].shape), _full(w["kvn"].shape),
            _full(w["wuq"].shape), _full(w["wuqr"].shape), _full(w["wuk"].shape), _full(w["wuv"].shape),
            pl.BlockSpec((tm, MLA_QK_PAD), pos), pl.BlockSpec((tm, MLA_QK_PAD), pos),
            pl.BlockSpec((tm, LANES), pos), pl.BlockSpec((tm, LANES), pos),
        ],
        out_specs=[pl.BlockSpec((tm, hq), row), pl.BlockSpec((tm, hq), row),
                   pl.BlockSpec((tm, MLA_HEADS * MLA_V), row)],
        out_shape=[jax.ShapeDtypeStruct((T, hq), BF16), jax.ShapeDtypeStruct((T, hq), BF16),
                   jax.ShapeDtypeStruct((T, MLA_HEADS * MLA_V), BF16)],
        compiler_params=_params("parallel"),
        name="mla_proj",
    )(x2d, w["wa"], w["qn"], w["kvn"], w["wuq"], w["wuqr"], w["wuk"], w["wuv"], cq, sq, ck, sk)


def _attn_kernel(q_ref, kt_ref, v_ref, o_ref, *, tk, unroll):
    q = q_ref[...]
    tq = q.shape[0]
    nk = kt_ref.shape[1] // tk

    def body(c, carry):
        m, l, acc = carry
        for u in range(unroll):
            off = pl.multiple_of((c * unroll + u) * tk, tk)
            s = _dot(q, kt_ref[:, pl.ds(off, tk)])
            m_new = jnp.maximum(m, jnp.max(s, axis=-1, keepdims=True))
            corr = jnp.exp2(m - m_new)
            p = jnp.exp2(s - m_new)
            l = corr * l + jnp.sum(p, axis=-1, keepdims=True)
            acc = corr * acc + _dot(p.astype(BF16), v_ref[pl.ds(off, tk), :])
            m = m_new
        return m, l, acc

    init = (jnp.full((tq, 1), -jnp.inf, F32), jnp.zeros((tq, 1), F32), jnp.zeros((tq, MLA_V), F32))
    _, l, acc = lax.fori_loop(0, nk // unroll, body, init)
    o_ref[...] = (acc / l).astype(o_ref.dtype)


def attention(q2d, kt2d, v2d, batch, seq, tq=512, tk=512, unroll=4):
    T = q2d.shape[0]
    tq, tk = math.gcd(seq, tq), math.gcd(seq, tk)
    unroll = math.gcd(seq // tk, unroll)
    nq = seq // tq
    return pl.pallas_call(
        functools.partial(_attn_kernel, tk=tk, unroll=unroll),
        grid=(batch, MLA_HEADS, nq),
        in_specs=[
            pl.BlockSpec((tq, MLA_QK_PAD), lambda b, h, i: (b * nq + i, h)),
            pl.BlockSpec((MLA_QK_PAD, seq), lambda b, h, i: (h, b)),
            pl.BlockSpec((seq, MLA_V), lambda b, h, i: (b, h)),
        ],
        out_specs=pl.BlockSpec((tq, MLA_V), lambda b, h, i: (b * nq + i, h)),
        out_shape=jax.ShapeDtypeStruct((T, MLA_HEADS * MLA_V), BF16),
        compiler_params=_params("parallel", "parallel", "arbitrary"),
        name="mla_attention",
    )(q2d, kt2d, v2d)


def _proj_res_ln_kernel(a_ref, x_ref, w_ref, g_ref, b_ref, o_ref, ob_ref, *, alpha):
    h = _dot(a_ref[...], w_ref[...])
    y = _layer_norm(alpha * x_ref[...] + h, g_ref[...], b_ref[...])
    o_ref[...] = y
    ob_ref[...] = y.astype(BF16)


def proj_res_ln(a2d, x2d, w, g, b, alpha, tm=512):
    T = x2d.shape[0]
    tm = math.gcd(T, tm)
    row = lambda i: (i, 0)
    return pl.pallas_call(
        functools.partial(_proj_res_ln_kernel, alpha=alpha),
        grid=(T // tm,),
        in_specs=[pl.BlockSpec((tm, a2d.shape[1]), row), pl.BlockSpec((tm, D_MODEL), row),
                  _full(w.shape), _full(g.shape), _full(b.shape)],
        out_specs=[pl.BlockSpec((tm, D_MODEL), row), pl.BlockSpec((tm, D_MODEL), row)],
        out_shape=[jax.ShapeDtypeStruct((T, D_MODEL), F32), jax.ShapeDtypeStruct((T, D_MODEL), BF16)],
        compiler_params=_params("parallel"),
        name="proj_res_ln",
    )(a2d, x2d, w, g, b)


def _matmul_kernel(x_ref, w_ref, o_ref):
    o_ref[...] = _dot(x_ref[...].astype(BF16), w_ref[...])


def matmul(x2d, w, tm=512, tn=1280):
    T, K = x2d.shape
    N = w.shape[1]
    tm = math.gcd(T, tm)
    return pl.pallas_call(
        _matmul_kernel,
        grid=(T // tm, N // tn),
        in_specs=[pl.BlockSpec((tm, K), lambda i, j: (i, 0)), pl.BlockSpec((K, tn), lambda i, j: (0, j))],
        out_specs=pl.BlockSpec((tm, tn), lambda i, j: (i, j)),
        out_shape=jax.ShapeDtypeStruct((T, N), F32),
        compiler_params=_params("parallel", "arbitrary"),
        name="matmul",
    )(x2d, w)


def _scan_tables():
    C, L = SCAN_CHUNK, SCAN_LEVELS
    mats = np.zeros((2, (2 + L) * C, C), np.float32)
    lvl = np.full((2, C, C), -1, np.int32)
    for d in range(2):
        p = np.arange(C) if d == 0 else C - 1 - np.arange(C)
        pt, pu = p[:, None], p[None, :]
        mats[d, 0:C] = pu <= pt
        mats[d, C:2 * C] = pu > pt
        for l in range(L):
            m = 1 << l
            r = (pt // (2 * m)) * (2 * m) + m - 1
            qside = (pt % (2 * m)) >= m
            e = np.where(qside, (pu > r) & (pu <= pt), (pu > pt) & (pu <= r))
            mats[d, (2 + l) * C:(3 + l) * C] = e
            same = (pt // (2 * m)) == (pu // (2 * m))
            lvl[d][same & qside & ((pu % (2 * m)) < m)] = l
        lvl[d][pt == pu] = L
    return jnp.asarray(np.concatenate([mats, mats], axis=2), BF16), jnp.asarray(np.concatenate([lvl, lvl], axis=2))


def _hgrn_scan_kernel(q_ref, z_ref, v_ref, lb_ref, mat_ref, lvl_ref, o_ref, state_ref, kk_ref, ex_ref):
    C, L = SCAN_CHUNK, SCAN_LEVELS

    @pl.when(pl.program_id(2) == 0)
    def _():
        state_ref[...] = jnp.zeros_like(state_ref)

    z = z_ref[...]
    lb = lb_ref[...]
    e = jnp.exp(-jnp.abs(z))
    r = 1.0 / (1.0 + e)
    er = e * r
    pos = z >= 0
    f = lb + (1.0 - lb) * jnp.where(pos, r, er)
    g = jnp.log(jnp.maximum(f, GATE_FLOOR))
    kk_ref[...] = (1.0 - lb) * jnp.where(pos, er, r)
    g1 = g.astype(BF16)
    g2 = (g - g1.astype(F32)).astype(BF16)
    ex_ref[...] = _dot(mat_ref[...], jnp.concatenate([g1, g2], axis=0))

    lvl = lvl_ref[...]
    zero = jnp.zeros((C, HGRN_DK), BF16)
    wide = 2 * HGRN_DK

    def block_diag(x):
        return jnp.concatenate([jnp.concatenate([x[:, :HGRN_DK], zero], axis=1),
                                jnp.concatenate([zero, x[:, HGRN_DK:]], axis=1)], axis=0)

    for hp in range(HGRN_HEADS // 2):
        sl = slice(hp * wide, (hp + 1) * wide)
        q = _silu(q_ref[:, sl])
        kk = kk_ref[:, sl]
        qb = q.astype(BF16)
        kb = kk.astype(BF16)
        attn = jnp.where(lvl == L, _dot_nt(qb, block_diag(kb)), 0.0)
        for l in range(L):
            xl = jnp.exp(ex_ref[(2 + l) * C:(3 + l) * C, sl]).astype(BF16)
            p = _dot_nt(qb * xl, block_diag(kb * xl))
            attn = jnp.where(lvl == l, p, attn)
        v2 = v_ref[:, sl]
        e_b = ex_ref[0:C, sl]
        e_last = ex_ref[C:2 * C, sl]
        b_last = e_b[0:1] + e_last[0:1]
        st = [state_ref[2 * hp + d] for d in range(2)]
        st_bd = jnp.concatenate([jnp.concatenate([st[0].astype(BF16), zero], axis=1),
                                 jnp.concatenate([zero, st[1].astype(BF16)], axis=1)], axis=0)
        o = _dot(attn.astype(BF16), block_diag(v2.astype(BF16)))
        o_ref[:, sl] = o + _dot_nt((q * jnp.exp(e_b)).astype(BF16), st_bd)
        kd = (kk * jnp.exp(e_last)).astype(BF16)
        for d in range(2):
            hs = slice(d * HGRN_DK, (d + 1) * HGRN_DK)
            vt = v2[:, hs].T.astype(BF16)
            state_ref[2 * hp + d] = st[d] * jnp.exp(b_last[:, hs]) + _dot(vt, kd[:, hs])


def hgrn_scan(proj, lb, batch, seq):
    T = proj.shape[0]
    C = SCAN_CHUNK
    nc = seq // C
    hk = HGRN_HEADS * HGRN_DK
    mats, lvl = _scan_tables()

    def rowblk(b, d, c):
        return b * nc + c + d * (nc - 1 - 2 * c)

    return pl.pallas_call(
        _hgrn_scan_kernel,
        grid=(batch, 2, nc),
        in_specs=[
            pl.BlockSpec((C, hk), lambda b, d, c: (rowblk(b, d, c), 0)),
            pl.BlockSpec((C, hk), lambda b, d, c: (rowblk(b, d, c), 1 + d)),
            pl.BlockSpec((C, hk), lambda b, d, c: (rowblk(b, d, c), 3)),
            pl.BlockSpec((None, 1, hk), lambda b, d, c: (d, 0, 0)),
            pl.BlockSpec((None,) + mats.shape[1:], lambda b, d, c: (d, 0, 0)),
            pl.BlockSpec((None, C, 2 * C), lambda b, d, c: (d, 0, 0)),
        ],
        out_specs=pl.BlockSpec((None, C, hk), lambda b, d, c: (d, rowblk(b, d, c), 0)),
        out_shape=jax.ShapeDtypeStruct((2, T, hk), F32),
        scratch_shapes=[pltpu.VMEM((HGRN_HEADS, HGRN_DV, HGRN_DK), F32), pltpu.VMEM((C, hk), F32),
                        pltpu.VMEM(((2 + SCAN_LEVELS) * C, hk), F32)],
        compiler_params=_params("parallel", "arbitrary", "arbitrary"),
        name="hgrn_scan",
    )(proj, proj, proj, lb, mats, lvl)


def _hgrn_out_kernel(o_ref, g_ref, x_ref, nw_ref, w_ref, lg_ref, lbias_ref, y_ref, yb_ref, a_ref, *, alpha):
    nw = nw_ref[...]
    for h in range(HGRN_HEADS):
        sl = slice(h * HGRN_DV, (h + 1) * HGRN_DV)
        o = o_ref[0, :, sl] + o_ref[1, :, sl]
        on = o * lax.rsqrt(jnp.mean(o * o, axis=-1, keepdims=True) + RMS_EPS) * nw
        a_ref[:, sl] = (on * _silu(g_ref[:, sl])).astype(BF16)
    hmix = _dot(a_ref[...], w_ref[...])
    y = _layer_norm(alpha * x_ref[...] + hmix, lg_ref[...], lbias_ref[...])
    y_ref[...] = y
    yb_ref[...] = y.astype(BF16)


def hgrn_out(o2, proj, x2d, nw, w, g, b, alpha, tm=256):
    T = x2d.shape[0]
    tm = math.gcd(T, tm)
    hv = HGRN_HEADS * HGRN_DV
    row = lambda i: (i, 0)
    return pl.pallas_call(
        functools.partial(_hgrn_out_kernel, alpha=alpha),
        grid=(T // tm,),
        in_specs=[pl.BlockSpec((2, tm, hv), lambda i: (0, i, 0)),
                  pl.BlockSpec((tm, hv), lambda i: (i, 4)),
                  pl.BlockSpec((tm, D_MODEL), row),
                  _full(nw.shape), _full(w.shape), _full(g.shape), _full(b.shape)],
        out_specs=[pl.BlockSpec((tm, D_MODEL), row), pl.BlockSpec((tm, D_MODEL), row)],
        out_shape=[jax.ShapeDtypeStruct((T, D_MODEL), F32), jax.ShapeDtypeStruct((T, D_MODEL), BF16)],
        scratch_shapes=[pltpu.VMEM((tm, hv), BF16)],
        compiler_params=_params("parallel"),
        name="hgrn_out",
    )(o2, proj, x2d, nw, w, g, b)


_PAIRS = [(a, b) for a in range(PEER_TOPK) for b in range(PEER_TOPK) if (a + 1) * (b + 1) <= PEER_TOPK]
_NPAIR_PAD = -(-len(_PAIRS) // 8) * 8


def _top_rows(s, k, payload=None):
    n = s.shape[0]
    iota = lax.broadcasted_iota(jnp.int32, s.shape, 0)
    sub = 8
    iota8 = lax.broadcasted_iota(jnp.int32, (sub, s.shape[1]), 0)
    vals, idxs = [], []
    for _ in range(k):
        nodes = [(s[r:r + sub], iota8 + r) for r in range(0, n, sub)]
        while len(nodes) > 1:
            nxt = [(jnp.maximum(va, vb), jnp.where(va >= vb, ia, ib))
                   for (va, ia), (vb, ib) in zip(nodes[0::2], nodes[1::2])]
            nodes = nxt + ([nodes[-1]] if len(nodes) % 2 else [])
        v8, i8 = nodes[0]
        m = jnp.max(v8, axis=0, keepdims=True)
        im = jnp.min(jnp.where(v8 == m, i8, n), axis=0, keepdims=True)
        hit = iota == im
        vals.append(m)
        if payload is None:
            idxs.append(im)
        else:
            idxs.append(jnp.max(jnp.where(hit, payload, -1), axis=0, keepdims=True))
        s = jnp.where(hit, -jnp.inf, s)
    return vals, idxs


def _retrieve_head(q_halves, keys_ref):
    sv, si = [], []
    for c in range(2):
        s = _dot_nt(keys_ref[c], q_halves[c])
        v_, i_ = _top_rows(s, PEER_TOPK)
        sv.append(v_)
        si.append(i_)
    return _combine_halves(sv, si)


def _combine_halves(sv, si):
    tm = sv[0][0].shape[1]
    neg = jnp.full((_NPAIR_PAD - len(_PAIRS), tm), -jnp.inf, F32)
    zero = jnp.zeros((_NPAIR_PAD - len(_PAIRS), tm), jnp.int32)
    cand = jnp.concatenate([sv[0][a] + sv[1][b] for a, b in _PAIRS] + [neg], axis=0)
    cid = jnp.concatenate([si[0][a] * N_KEYS + si[1][b] for a, b in _PAIRS] + [zero], axis=0)
    tv, te = _top_rows(cand, PEER_TOPK, payload=cid)
    tv = jnp.concatenate(tv, axis=0)
    ex = jnp.exp(tv - tv[0:1])
    return ex / jnp.sum(ex, axis=0, keepdims=True), jnp.concatenate(te, axis=0)


def _peer_topk_kernel(xb_ref, wq_ref, keys_ref, gate_ref, eid_ref, q_ref):
    q_ref[...] = _dot(xb_ref[...], wq_ref[...]).astype(BF16)
    half = PEER_DK // 2

    def head(h, carry):
        off = pl.multiple_of(h * PEER_DK, PEER_DK)
        gate, eid = _retrieve_head([q_ref[:, pl.ds(off + c * half, half)] for c in range(2)], keys_ref)
        row = pl.multiple_of(h * PEER_TOPK, PEER_TOPK)
        gate_ref[pl.ds(row, PEER_TOPK), :] = gate
        eid_ref[pl.ds(row, PEER_TOPK), :] = eid
        return carry

    lax.fori_loop(0, PEER_HEADS, head, 0)


def peer_topk(xb, wq, keys, tm=256):
    T = xb.shape[0]
    tm = math.gcd(T, tm)
    ns = PEER_HEADS * PEER_TOPK
    return pl.pallas_call(
        _peer_topk_kernel,
        grid=(T // tm,),
        in_specs=[pl.BlockSpec((tm, D_MODEL), lambda i: (i, 0)), _full(wq.shape), _full(keys.shape)],
        out_specs=[pl.BlockSpec((ns, tm), lambda i: (0, i)), pl.BlockSpec((ns, tm), lambda i: (0, i))],
        out_shape=[jax.ShapeDtypeStruct((ns, T), F32), jax.ShapeDtypeStruct((ns, T), jnp.int32)],
        scratch_shapes=[pltpu.VMEM((tm, PEER_HEADS * PEER_DK), BF16)],
        compiler_params=_params("parallel"),
        name="peer_topk",
    )(xb, wq, keys)


def _selection_rows(gate_ref, eid_ref, gt_ref, it_ref, jt_ref):
    eid_t = eid_ref[...].T
    gt_ref[...] = gate_ref[...].T
    it_ref[...] = lax.shift_right_logical(eid_t, int(math.log2(N_KEYS)))
    jt_ref[...] = lax.bitwise_and(eid_t, N_KEYS - 1)


def _gate_tiles(gt_ref, it_ref, jt_ref, tile_ref, sel0, tile0):
    ns = gt_ref.shape[1]
    iota = lax.broadcasted_iota(jnp.int32, (N_KEYS, ns), 0)
    iota_pad = lax.broadcasted_iota(jnp.int32, (N_KEYS + W_LHS_PAD, ns), 0)
    zero = jnp.zeros((N_KEYS, ns), BF16)
    g8 = gt_ref[pl.ds(sel0, W_GROUP), :]
    i8 = it_ref[pl.ds(sel0, W_GROUP), :]
    j8 = jt_ref[pl.ds(sel0, W_GROUP), :]
    for u in range(0, W_GROUP, 2):
        at0 = jnp.where(iota_pad == i8[u:u + 1], g8[u:u + 1], 0.0).astype(BF16)
        at1 = jnp.where(iota_pad == i8[u + 1:u + 2] + W_PAD, g8[u + 1:u + 2], 0.0).astype(BF16)
        bt = [jnp.where(iota == j8[u + d:u + d + 1], 1.0, 0.0).astype(BF16) for d in range(2)]
        lhs = jnp.concatenate([at0, at1], axis=1)
        rhs = jnp.concatenate([jnp.concatenate([bt[0], zero], axis=1),
                               jnp.concatenate([zero, bt[1]], axis=1)], axis=0)
        res = _dot_nt(lhs, rhs)
        row = pl.multiple_of((tile0 + u) * W_PITCH, 8)
        tile_ref[pl.ds(row, N_KEYS), :] = res[:N_KEYS, :N_KEYS]
        tile_ref[pl.ds(row + N_KEYS, N_KEYS + 2 * W_PAD), :] = res[:N_KEYS + 2 * W_PAD, N_KEYS:]


def _tile_rows_out(tile_ref, w_ref, ntok, i):
    col = tile_ref[pl.ds(i, ntok, stride=W_PITCH), :]
    lane0 = i * N_KEYS
    if not isinstance(i, int):
        lane0 = pl.multiple_of(lane0, N_KEYS)
    w_ref[:, pl.ds(lane0, N_KEYS)] = col.astype(BF16)


def _peer_w_kernel(gate_ref, eid_ref, w_ref, gt_ref, it_ref, jt_ref, tile_ref):
    tm = gate_ref.shape[1]
    _selection_rows(gate_ref, eid_ref, gt_ref, it_ref, jt_ref)

    def token_group(p, carry):
        base = pl.multiple_of(p * W_GROUP, W_GROUP)
        _gate_tiles(gt_ref, it_ref, jt_ref, tile_ref, base, base)
        return carry

    lax.fori_loop(0, tm // W_GROUP, token_group, 0)

    def key_rows(i2, carry):
        for d in range(W_ROWS):
            _tile_rows_out(tile_ref, w_ref, tm, i2 * W_ROWS + d)
        return carry

    lax.fori_loop(0, N_KEYS // W_ROWS, key_rows, 0)


def peer_w(gate, eid, tm=128):
    ns, T = gate.shape
    tm = math.gcd(T, tm)
    return pl.pallas_call(
        _peer_w_kernel,
        grid=(T // tm,),
        in_specs=[pl.BlockSpec((ns, tm), lambda i: (0, i)), pl.BlockSpec((ns, tm), lambda i: (0, i))],
        out_specs=pl.BlockSpec((tm, N_EXPERTS), lambda i: (i, 0)),
        out_shape=jax.ShapeDtypeStruct((T, N_EXPERTS), BF16),
        scratch_shapes=[pltpu.VMEM((tm, ns), F32), pltpu.VMEM((tm, ns), jnp.int32),
                        pltpu.VMEM((tm, ns), jnp.int32), pltpu.VMEM((tm * W_PITCH, N_KEYS), F32)],
        compiler_params=_params("parallel"),
        name="peer_gate_matrix",
    )(gate, eid)


def _dense_init(o_ref):
    @pl.when(pl.program_id(1) == 0)
    def _():
        o_ref[...] = jnp.zeros_like(o_ref)


def _dense_step(xb_ref, ut_ref, v_ref, w_ref, o_ref):
    h = _dot(xb_ref[...], ut_ref[...])
    act = 0.5 * h * (1.0 + lax.erf(h * (1.0 / math.sqrt(2.0))))
    hw = act * w_ref[...].astype(F32)
    o_ref[...] += _dot(hw.astype(BF16), v_ref[...])
    return hw


def _after(x, marker):
    z = pltpu.bitcast(marker, jnp.uint32)
    z = lax.shift_right_logical(lax.shift_right_logical(z, jnp.uint32(31)), jnp.uint32(1))
    z = pltpu.bitcast(z, F32)
    return x + jnp.tile(z, (x.shape[0] // z.shape[0], x.shape[1] // z.shape[1]))


def _peer_dense_kernel(xb_ref, ut_ref, v_ref, w_ref, o_ref):
    _dense_init(o_ref)
    _dense_step(xb_ref, ut_ref, v_ref, w_ref, o_ref)


def _peer_fused_kernel(*refs, with_gate, with_topk):
    refs = list(refs)
    xb_ref, ut_ref, v_ref, w_ref = refs[:4]
    del refs[:4]
    if with_gate:
        sg_ref, se_ref = refs[:2]
        del refs[:2]
    if with_topk:
        xn_ref, wq_ref, keys_ref = refs[:3]
        del refs[:3]
    o_ref = refs.pop(0)
    if with_gate:
        wn_ref = refs.pop(0)
    if with_topk:
        gate_ref, eid_ref = refs[:2]
        del refs[:2]
    if with_gate:
        gt_ref, it_ref, jt_ref, tile_ref = refs
    j = pl.program_id(1)

    @pl.when(j == 0)
    def _():
        o_ref[...] = jnp.zeros_like(o_ref)
        if with_gate:
            _selection_rows(sg_ref, se_ref, gt_ref, it_ref, jt_ref)

    if with_topk:
        half = PEER_DK // 2
        q = _dot(xn_ref[...], wq_ref[...]).astype(BF16)
        scores = [_dot_nt(keys_ref[c], q[:, c * half:(c + 1) * half]) for c in range(2)]
    hw = _dense_step(xb_ref, ut_ref, v_ref, w_ref, o_ref)
    if with_gate:
        ntok = wn_ref.shape[0]
        sel0 = pl.multiple_of(j * ntok, ntok)
        for g in range(0, ntok, W_GROUP):
            _gate_tiles(gt_ref, it_ref, jt_ref, tile_ref, sel0 + g, g)
        for i in range(N_KEYS):
            _tile_rows_out(tile_ref, wn_ref, ntok, i)
    if with_topk:
        marker = hw[hw.shape[0] - 8:, :LANES]
        row = pl.multiple_of(j * PEER_TOPK, PEER_TOPK)
        tm = xn_ref.shape[0]
        sub = math.gcd(tm, FUSE_LANES)
        for t0 in range(0, tm, sub):
            top = [_top_rows(_after(scores[c][:, t0:t0 + sub], marker), PEER_TOPK) for c in range(2)]
            gate, eid = _combine_halves([top[0][0], top[1][0]], [top[0][1], top[1][1]])
            gate_ref[pl.ds(row, PEER_TOPK), t0:t0 + sub] = gate
            eid_ref[pl.ds(row, PEER_TOPK), t0:t0 + sub] = eid
            marker = gate[:8]


def _dense_specs(tm, ne):
    return [pl.BlockSpec((tm, D_MODEL), lambda i, j: (i, 0)),
            pl.BlockSpec((D_MODEL, ne), lambda i, j: (0, j)),
            pl.BlockSpec((ne, D_MODEL), lambda i, j: (j, 0)),
            pl.BlockSpec((tm, ne), lambda i, j: (i, j))]


def peer_dense(xb, ut, v, wmat, tm=512, ne=N_EXPERTS // PEER_HEADS):
    T = xb.shape[0]
    tm = math.gcd(T, tm)
    return pl.pallas_call(
        _peer_dense_kernel,
        grid=(T // tm, N_EXPERTS // ne),
        in_specs=_dense_specs(tm, ne),
        out_specs=pl.BlockSpec((tm, D_MODEL), lambda i, j: (i, 0)),
        out_shape=jax.ShapeDtypeStruct((T, D_MODEL), F32),
        compiler_params=_params("parallel", "arbitrary"),
        name="peer_dense",
    )(xb, ut, v, wmat)


def peer_fused(xb, ut, v, wmat, sel_next=None, xn=None, wq=None, keys=None, tm=512):
    T = xb.shape[0]
    tm = math.gcd(T, tm)
    ne = N_EXPERTS // PEER_HEADS
    ns = PEER_HEADS * PEER_TOPK
    with_gate, with_topk = sel_next is not None, xn is not None
    ntok = tm // PEER_HEADS
    tile = lambda i, j: (i, 0)
    sel = lambda i, j: (0, i)
    args = [xb, ut, v, wmat]
    in_specs = _dense_specs(tm, ne)
    out_specs = [pl.BlockSpec((tm, D_MODEL), tile)]
    out_shape = [jax.ShapeDtypeStruct((T, D_MODEL), F32)]
    scratch = []
    if with_gate:
        assert sel_next[0].shape == (ns, T) and ntok % W_GROUP == 0
        args += list(sel_next)
        in_specs += [pl.BlockSpec((ns, tm), sel), pl.BlockSpec((ns, tm), sel)]
        out_specs.append(pl.BlockSpec((ntok, N_EXPERTS), lambda i, j: (i * PEER_HEADS + j, 0)))
        out_shape.append(jax.ShapeDtypeStruct((T, N_EXPERTS), BF16))
        scratch = [pltpu.VMEM((tm, ns), F32), pltpu.VMEM((tm, ns), jnp.int32), pltpu.VMEM((tm, ns), jnp.int32),
                   pltpu.VMEM((ntok * W_PITCH, N_KEYS), F32)]
    if with_topk:
        assert xn.shape[0] == T
        args += [xn, wq, keys]
        in_specs += [pl.BlockSpec((tm, D_MODEL), tile), pl.BlockSpec((D_MODEL, PEER_DK), lambda i, j: (0, j)),
                     _full(keys.shape)]
        out_specs += [pl.BlockSpec((ns, tm), sel), pl.BlockSpec((ns, tm), sel)]
        out_shape += [jax.ShapeDtypeStruct((ns, T), F32), jax.ShapeDtypeStruct((ns, T), jnp.int32)]
    return pl.pallas_call(
        functools.partial(_peer_fused_kernel, with_gate=with_gate, with_topk=with_topk),
        grid=(T // tm, PEER_HEADS),
        in_specs=in_specs,
        out_specs=out_specs,
        out_shape=out_shape,
        scratch_shapes=scratch,
        compiler_params=_params("parallel", "arbitrary"),
        name="peer_fused",
    )(*args)


def _ln_ple_kernel(x_ref, f_ref, p_ref, g_ref, b_ref, wg_ref, bg_ref, wp_ref, o_ref, *, alpha):
    y = _layer_norm(alpha * x_ref[...] + f_ref[...], g_ref[...], b_ref[...])
    gate = 1.0 / (1.0 + jnp.exp(-(_dot(y.astype(BF16), wg_ref[...]) + bg_ref[...])))
    o_ref[...] = y + gate * _dot(p_ref[...].astype(BF16), wp_ref[...])


def ln_ple(x2d, f2d, p2d, p_row0, g, b, wg, bg, wp, alpha, tm=512):
    T = x2d.shape[0]
    tm = math.gcd(math.gcd(T, tm), p_row0) if p_row0 else math.gcd(T, tm)
    blk0 = p_row0 // tm
    row = lambda i: (i, 0)
    return pl.pallas_call(
        functools.partial(_ln_ple_kernel, alpha=alpha),
        grid=(T // tm,),
        in_specs=[pl.BlockSpec((tm, D_MODEL), row), pl.BlockSpec((tm, D_MODEL), row),
                  pl.BlockSpec((tm, PLE_DIM), lambda i: (i + blk0, 0)),
                  _full(g.shape), _full(b.shape), _full(wg.shape), _full(bg.shape), _full(wp.shape)],
        out_specs=pl.BlockSpec((tm, D_MODEL), row),
        out_shape=jax.ShapeDtypeStruct((T, D_MODEL), F32),
        compiler_params=_params("parallel"),
        name="ln_ple",
    )(x2d, f2d, p2d, g, b, wg, bg, wp)


def _rot_cols(w):
    half = MLA_ROPE // 2
    return jnp.concatenate([-w[..., half:], w[..., :half]], axis=-1)


def _mla_weights(w_a, q_norm, kv_norm, w_uq, w_ukv):
    zk = jnp.zeros((D_MODEL, LANES - MLA_ROPE), F32)
    w_kr = w_a[:, MLA_Q_LORA + MLA_KV_LORA:]
    wa = jnp.concatenate([w_a[:, :MLA_Q_LORA + MLA_KV_LORA], w_kr, zk, _rot_cols(w_kr), zk], axis=1)
    uq = w_uq.reshape(MLA_Q_LORA, MLA_HEADS, MLA_NOPE + MLA_ROPE)
    zq = jnp.zeros((MLA_Q_LORA, MLA_HEADS, MLA_QK_PAD - MLA_NOPE - MLA_ROPE), F32)
    wuq = jnp.concatenate([uq, zq], axis=-1).reshape(MLA_Q_LORA, MLA_HEADS * MLA_QK_PAD)
    wuqr = jnp.concatenate([jnp.zeros_like(uq[..., :MLA_NOPE]), _rot_cols(uq[..., MLA_NOPE:]), zq],
                           axis=-1).reshape(MLA_Q_LORA, MLA_HEADS * MLA_QK_PAD)
    ukv = w_ukv.reshape(MLA_KV_LORA, MLA_HEADS, MLA_NOPE + MLA_V)
    return {
        "wa": wa.astype(BF16),
        "qn": q_norm.astype(F32).reshape(1, -1),
        "kvn": kv_norm.astype(F32).reshape(1, -1),
        "wuq": wuq.astype(BF16),
        "wuqr": wuqr.astype(BF16),
        "wuk": ukv[..., :MLA_NOPE].reshape(MLA_KV_LORA, -1).astype(BF16),
        "wuv": ukv[..., MLA_NOPE:].reshape(MLA_KV_LORA, -1).astype(BF16),
    }


def _rope_tabs(seq):
    inv_freq = 1.0 / (ROPE_THETA ** (jnp.arange(0, MLA_ROPE, 2, dtype=F32) / MLA_ROPE))
    ang = jnp.arange(seq, dtype=F32)[:, None] * inv_freq[None, :]
    emb = jnp.concatenate([ang, ang], axis=-1)
    cos, sin = jnp.cos(emb), jnp.sin(emb)
    one = jnp.ones((seq, MLA_NOPE), F32)
    z64 = jnp.zeros((seq, LANES - MLA_ROPE), F32)
    z128 = jnp.zeros((seq, MLA_NOPE), F32)
    return (jnp.concatenate([one, cos, z64], axis=1), jnp.concatenate([z128, sin, z64], axis=1),
            jnp.concatenate([cos, z64], axis=1), jnp.concatenate([sin, z64], axis=1))


def _mixer(i, x2d, batch, seq, lbs, mla_w, hgrn_w, ln_w, alpha, tabs):
    j = i // 2
    g1, b1, _, _ = ln_w[i]
    if i % 2 == 0:
        w = mla_w[j]
        q2d, k2d, v2d = mla_proj(x2d, seq, w, tabs)
        o2d = attention(q2d, k2d.T, v2d, batch, seq)
        return proj_res_ln(o2d, x2d, w["wo"], g1, b1, alpha)
    w = hgrn_w[j]
    proj = matmul(x2d, w["win"])
    o2 = hgrn_scan(proj, lbs[:, j], batch, seq)
    return hgrn_out(o2, proj, x2d, w["nw"], w["wo"], g1, b1, alpha)


def _peer_groups(x1b, pw):
    n = len(x1b)
    if n > 1 and all(x.shape[0] == x1b[0].shape[0] for x in x1b):
        sel = {k: peer_topk(x1b[k], pw["wq"], pw["keys"]) for k in range(2)}
        wmat = {0: peer_w(*sel[0])}
        out = []
        for k in range(n):
            sel_next = sel[k + 1] if k + 1 < n else None
            xn = x1b[k + 2] if k + 2 < n else None
            res = list(peer_fused(x1b[k], pw["ut"], pw["v"], wmat[k], sel_next, xn, pw["wq"], pw["keys"]))
            out.append(res.pop(0))
            if sel_next is not None:
                wmat[k + 1] = res.pop(0)
            if xn is not None:
                sel[k + 2] = (res[0], res[1])
        return out
    return [peer_dense(x, pw["ut"], pw["v"], peer_w(*peer_topk(x, pw["wq"], pw["keys"]))) for x in x1b]


def _trunk(groups, lbs, mla_w, hgrn_w, peer_w_, ln_w, ple_w, depth):
    alpha = (2 * depth) ** 0.25
    tabs = {g["seq"]: None for g in groups}
    for s in tabs:
        tabs[s] = _rope_tabs(s)
    xs = [g["x"] for g in groups]
    for i in range(depth):
        mixed = [_mixer(i, x, g["batch"], g["seq"], lbs, mla_w, hgrn_w, ln_w, alpha, tabs[g["seq"]])
                 for x, g in zip(xs, groups)]
        fs = _peer_groups([m[1] for m in mixed], peer_w_[i])
        _, _, g2, b2 = ln_w[i]
        wg, bg, wp = ple_w[i]
        xs = [ln_ple(m[0], f, g["p"][i], g["row0"], g2, b2, wg, bg, wp, alpha)
              for m, f, g in zip(mixed, fs, groups)]
    return xs


def _token_groups(x, p, target):
    batch, seq, _ = x.shape
    depth = p.shape[0]
    T = batch * seq
    x2d = x.reshape(T, D_MODEL)
    p2d = p.reshape(depth, T, PLE_DIM)
    n = T // target if (target % seq == 0 and T % target == 0) else 1
    tg = T // n
    return [{"x": x2d if n == 1 else x2d[k * tg:(k + 1) * tg], "p": p2d, "row0": k * tg,
             "batch": batch // n, "seq": seq} for k in range(n)]


def kernel(x_prompt, x_sample, p_prompt, p_sample, mla_w_a, mla_q_norm, mla_kv_norm, mla_w_uq, mla_w_ukv, mla_w_o, hgrn_w_in, hgrn_lb, hgrn_norm, hgrn_w_o, peer_w_q, peer_sub_keys, peer_u, peer_v, ln1_g, ln1_b, ln2_g, ln2_b, ple_gate_w, ple_gate_b, ple_proj):
    depth = peer_w_q.shape[0]
    lbs = lower_bounds(hgrn_lb)
    mla_w = []
    for j in range(mla_w_a.shape[0]):
        w = _mla_weights(mla_w_a[j], mla_q_norm[j], mla_kv_norm[j], mla_w_uq[j], mla_w_ukv[j])
        w["wo"] = mla_w_o[j].astype(BF16)
        mla_w.append(w)
    hgrn_w = [{"win": hgrn_w_in[j].astype(BF16), "nw": hgrn_norm[j].astype(F32).reshape(1, -1),
               "wo": hgrn_w_o[j].astype(BF16)} for j in range(hgrn_w_in.shape[0])]
    peer_w_ = [{"wq": peer_w_q[i].astype(BF16), "keys": peer_sub_keys[i].astype(BF16),
                "ut": peer_u[i].astype(BF16).T, "v": peer_v[i].astype(BF16)} for i in range(depth)]
    r = lambda a: a.astype(F32).reshape(1, -1)
    ln_w = [(r(ln1_g[i]), r(ln1_b[i]), r(ln2_g[i]), r(ln2_b[i])) for i in range(depth)]
    ple_w = [(ple_gate_w[i].astype(BF16), r(ple_gate_b[i]), ple_proj[i].astype(BF16)) for i in range(depth)]
    target = max(x_prompt.shape[1], x_sample.shape[1])
    g_prompt = _token_groups(x_prompt, p_prompt, target)
    groups = g_prompt + _token_groups(x_sample, p_sample, target)
    ys = _trunk(groups, lbs, mla_w, hgrn_w, peer_w_, ln_w, ple_w, depth)
    join = lambda parts: parts[0] if len(parts) == 1 else jnp.concatenate(parts, axis=0)
    y_prompt = join(ys[:len(g_prompt)]).reshape(x_prompt.shape)
    y_sample = join(ys[len(g_prompt):]).reshape(x_sample.shape)
    return (y_prompt, y_sample)
```

```python
import functools
import math

import numpy as np
import jax
import jax.numpy as jnp
from jax import lax
from jax.experimental import pallas as pl
from jax.experimental.pallas import tpu as pltpu

D_MODEL = 1024
PLE_DIM = 256
MLA_HEADS = 8
MLA_NOPE = 128
MLA_ROPE = 64
MLA_V = 128
MLA_Q_LORA = 384
MLA_KV_LORA = 256
MLA_SCALE = (MLA_NOPE + MLA_ROPE) ** -0.5
MLA_QK_PAD = 256
LOG2E = math.log2(math.e)
ROPE_THETA = 10000.0
HGRN_HEADS = 8
HGRN_DK = 128
HGRN_DV = 128
GATE_FLOOR = 1e-30
PEER_HEADS = 8
PEER_DK = 256
N_KEYS = 128
N_EXPERTS = N_KEYS * N_KEYS
PEER_TOPK = 16
LN_EPS = 1e-5
RMS_EPS = 1e-6

LANES = 128
VMEM_LIMIT = 56 * 1024 * 1024

SCAN_CHUNK = 128
SCAN_LEVELS = 7
W_PAD = 4
W_PITCH = N_KEYS + W_PAD
W_LHS_PAD = 16
W_GROUP = 32
W_ROWS = 8
FUSE_LANES = 256

F32 = jnp.float32
BF16 = jnp.bfloat16

_NT = (((1,), (1,)), ((), ()))


def _params(*sem):
    return pltpu.CompilerParams(dimension_semantics=sem, vmem_limit_bytes=VMEM_LIMIT)


def _dot(a, b):
    return jnp.dot(a, b, preferred_element_type=F32)


def _dot_nt(a, b):
    return lax.dot_general(a, b, _NT, preferred_element_type=F32)


def _layer_norm(y, g, b):
    mu = jnp.mean(y, axis=-1, keepdims=True)
    yc = y - mu
    var = jnp.mean(yc * yc, axis=-1, keepdims=True)
    return yc * lax.rsqrt(var + LN_EPS) * g + b


def _silu(t):
    return t * (1.0 / (1.0 + jnp.exp(-t)))


def _full(shape):
    nd = len(shape)
    return pl.BlockSpec(shape, lambda *_: (0,) * nd)


def _lower_bounds_kernel(lb_ref, out_ref):
    n = lb_ref.shape[1]
    for d in range(lb_ref.shape[0]):
        rows = [lb_ref[d, j] for j in range(n)]
        m = rows[0]
        for r in rows[1:]:
            m = jnp.maximum(m, r)
        ex = [jnp.exp(r - m) for r in rows]
        tot = ex[0]
        for e in ex[1:]:
            tot = tot + e
        sm = [e / tot for e in ex]
        cum = sm[0]
        out_ref[d, 0] = cum - sm[0]
        for j in range(1, n):
            cum = cum + sm[j]
            out_ref[d, j] = cum - sm[0]


def lower_bounds(hgrn_lb):
    two, n, hk = hgrn_lb.shape
    lb4 = hgrn_lb.astype(F32).reshape(two, n, 1, hk)
    out = pl.pallas_call(
        _lower_bounds_kernel,
        out_shape=jax.ShapeDtypeStruct((two, n, 1, hk), F32),
        name="hgrn_lower_bounds",
    )(lb4)
    return out


def _mla_proj_kernel(x_ref, wa_ref, qn_ref, kvn_ref, wuq_ref, wuqr_ref, wuk_ref, wuv_ref,
                     cq_ref, sq_ref, ck_ref, sk_ref, q_ref, k_ref, v_ref):
    xb = x_ref[...].astype(BF16)
    a = _dot(xb, wa_ref[...])
    c_q = a[:, :MLA_Q_LORA]
    c_kv = a[:, MLA_Q_LORA:MLA_Q_LORA + MLA_KV_LORA]
    kr = a[:, 640:768]
    krr = a[:, 768:896]
    cqn = (c_q * lax.rsqrt(jnp.mean(c_q * c_q, axis=-1, keepdims=True) + RMS_EPS) * qn_ref[...]).astype(BF16)
    ckvn = (c_kv * lax.rsqrt(jnp.mean(c_kv * c_kv, axis=-1, keepdims=True) + RMS_EPS) * kvn_ref[...]).astype(BF16)
    q = _dot(cqn, wuq_ref[...])
    qr = _dot(cqn, wuqr_ref[...])
    cq = cq_ref[...]
    sq = sq_ref[...]
    kro = (kr * ck_ref[...] + krr * sk_ref[...]).astype(BF16)
    kn = _dot(ckvn, wuk_ref[...]).astype(BF16)
    v_ref[...] = _dot(ckvn, wuv_ref[...]).astype(BF16)
    for h in range(MLA_HEADS):
        lo = h * MLA_QK_PAD
        qh = (q[:, lo:lo + MLA_QK_PAD] * cq + qr[:, lo:lo + MLA_QK_PAD] * sq) * (MLA_SCALE * LOG2E)
        q_ref[:, lo:lo + MLA_QK_PAD] = qh.astype(BF16)
        k_ref[:, lo:lo + MLA_NOPE] = kn[:, h * MLA_NOPE:(h + 1) * MLA_NOPE]
        k_ref[:, lo + MLA_NOPE:lo + MLA_QK_PAD] = kro


def mla_proj(x2d, seq, w, tabs, tm=256):
    T = x2d.shape[0]
    tm = math.gcd(seq, tm)
    nper = seq // tm
    cq, sq, ck, sk = tabs
    row = lambda i: (i, 0)
    pos = lambda i: (i % nper, 0)
    hq = MLA_HEADS * MLA_QK_PAD
    return pl.pallas_call(
        _mla_proj_kernel,
        grid=(T // tm,),
        in_specs=[
            pl.BlockSpec((tm, D_MODEL), row),
            _full(w["wa"].shape), _full(w["qn"].shape), _full(w["kvn"].shape),
            _full(w["wuq"].shape), _full(w["wuqr"].shape), _full(w["wuk"].shape), _full(w["wuv"].shape),
            pl.BlockSpec((tm, MLA_QK_PAD), pos), pl.BlockSpec((tm, MLA_QK_PAD), pos),
            pl.BlockSpec((tm, LANES), pos), pl.BlockSpec((tm, LANES), pos),
        ],
        out_specs=[pl.BlockSpec((tm, hq), row), pl.BlockSpec((tm, hq), row),
                   pl.BlockSpec((tm, MLA_HEADS * MLA_V), row)],
        out_shape=[jax.ShapeDtypeStruct((T, hq), BF16), jax.ShapeDtypeStruct((T, hq), BF16),
                   jax.ShapeDtypeStruct((T, MLA_HEADS * MLA_V), BF16)],
        compiler_params=_params("parallel"),
        name="mla_proj",
    )(x2d, w["wa"], w["qn"], w["kvn"], w["wuq"], w["wuqr"], w["wuk"], w["wuv"], cq, sq, ck, sk)


def _attn_kernel(q_ref, kt_ref, v_ref, o_ref, *, tk, unroll):
    q = q_ref[...]
    tq = q.shape[0]
    nk = kt_ref.shape[1] // tk

    def body(c, carry):
        m, l, acc = carry
        for u in range(unroll):
            off = pl.multiple_of((c * unroll + u) * tk, tk)
            s = _dot(q, kt_ref[:, pl.ds(off, tk)])
            m_new = jnp.maximum(m, jnp.max(s, axis=-1, keepdims=True))
            corr = jnp.exp2(m - m_new)
            p = jnp.exp2(s - m_new)
            l = corr * l + jnp.sum(p, axis=-1, keepdims=True)
            acc = corr * acc + _dot(p.astype(BF16), v_ref[pl.ds(off, tk), :])
            m = m_new
        return m, l, acc

    init = (jnp.full((tq, 1), -jnp.inf, F32), jnp.zeros((tq, 1), F32), jnp.zeros((tq, MLA_V), F32))
    _, l, acc = lax.fori_loop(0, nk // unroll, body, init)
    o_ref[...] = (acc / l).astype(o_ref.dtype)


def attention(q2d, kt2d, v2d, batch, seq, tq=512, tk=512, unroll=4):
    T = q2d.shape[0]
    tq, tk = math.gcd(seq, tq), math.gcd(seq, tk)
    unroll = math.gcd(seq // tk, unroll)
    nq = seq // tq
    return pl.pallas_call(
        functools.partial(_attn_kernel, tk=tk, unroll=unroll),
        grid=(batch, MLA_HEADS, nq),
        in_specs=[
            pl.BlockSpec((tq, MLA_QK_PAD), lambda b, h, i: (b * nq + i, h)),
            pl.BlockSpec((MLA_QK_PAD, seq), lambda b, h, i: (h, b)),
            pl.BlockSpec((seq, MLA_V), lambda b, h, i: (b, h)),
        ],
        out_specs=pl.BlockSpec((tq, MLA_V), lambda b, h, i: (b * nq + i, h)),
        out_shape=jax.ShapeDtypeStruct((T, MLA_HEADS * MLA_V), BF16),
        compiler_params=_params("parallel", "parallel", "arbitrary"),
        name="mla_attention",
    )(q2d, kt2d, v2d)


def _proj_res_ln_kernel(a_ref, x_ref, w_ref, g_ref, b_ref, o_ref, ob_ref, *, alpha):
    h = _dot(a_ref[...], w_ref[...])
    y = _layer_norm(alpha * x_ref[...] + h, g_ref[...], b_ref[...])
    o_ref[...] = y
    ob_ref[...] = y.astype(BF16)


def proj_res_ln(a2d, x2d, w, g, b, alpha, tm=512):
    T = x2d.shape[0]
    tm = math.gcd(T, tm)
    row = lambda i: (i, 0)
    return pl.pallas_call(
        functools.partial(_proj_res_ln_kernel, alpha=alpha),
        grid=(T // tm,),
        in_specs=[pl.BlockSpec((tm, a2d.shape[1]), row), pl.BlockSpec((tm, D_MODEL), row),
                  _full(w.shape), _full(g.shape), _full(b.shape)],
        out_specs=[pl.BlockSpec((tm, D_MODEL), row), pl.BlockSpec((tm, D_MODEL), row)],
        out_shape=[jax.ShapeDtypeStruct((T, D_MODEL), F32), jax.ShapeDtypeStruct((T, D_MODEL), BF16)],
        compiler_params=_params("parallel"),
        name="proj_res_ln",
    )(a2d, x2d, w, g, b)


def _matmul_kernel(x_ref, w_ref, o_ref):
    o_ref[...] = _dot(x_ref[...].astype(BF16), w_ref[...])


def matmul(x2d, w, tm=512, tn=1280):
    T, K = x2d.shape
    N = w.shape[1]
    tm = math.gcd(T, tm)
    return pl.pallas_call(
        _matmul_kernel,
        grid=(T // tm, N // tn),
        in_specs=[pl.BlockSpec((tm, K), lambda i, j: (i, 0)), pl.BlockSpec((K, tn), lambda i, j: (0, j))],
        out_specs=pl.BlockSpec((tm, tn), lambda i, j: (i, j)),
        out_shape=jax.ShapeDtypeStruct((T, N), F32),
        compiler_params=_params("parallel", "arbitrary"),
        name="matmul",
    )(x2d, w)


def _scan_tables():
    C, L = SCAN_CHUNK, SCAN_LEVELS
    mats = np.zeros((2, (2 + L) * C, C), np.float32)
    lvl = np.full((2, C, C), -1, np.int32)
    for d in range(2):
        p = np.arange(C) if d == 0 else C - 1 - np.arange(C)
        pt, pu = p[:, None], p[None, :]
        mats[d, 0:C] = pu <= pt
        mats[d, C:2 * C] = pu > pt
        for l in range(L):
            m = 1 << l
            r = (pt // (2 * m)) * (2 * m) + m - 1
            qside = (pt % (2 * m)) >= m
            e = np.where(qside, (pu > r) & (pu <= pt), (pu > pt) & (pu <= r))
            mats[d, (2 + l) * C:(3 + l) * C] = e
            same = (pt // (2 * m)) == (pu // (2 * m))
            lvl[d][same & qside & ((pu % (2 * m)) < m)] = l
        lvl[d][pt == pu] = L
    return jnp.asarray(np.concatenate([mats, mats], axis=2), BF16), jnp.asarray(np.concatenate([lvl, lvl], axis=2))


def _hgrn_scan_kernel(q_ref, z_ref, v_ref, lb_ref, mat_ref, lvl_ref, o_ref, state_ref, kk_ref, ex_ref):
    C, L = SCAN_CHUNK, SCAN_LEVELS

    @pl.when(pl.program_id(2) == 0)
    def _():
        state_ref[...] = jnp.zeros_like(state_ref)

    z = z_ref[...]
    lb = lb_ref[...]
    e = jnp.exp(-jnp.abs(z))
    r = 1.0 / (1.0 + e)
    er = e * r
    pos = z >= 0
    f = lb + (1.0 - lb) * jnp.where(pos, r, er)
    g = jnp.log(jnp.maximum(f, GATE_FLOOR))
    kk_ref[...] = (1.0 - lb) * jnp.where(pos, er, r)
    g1 = g.astype(BF16)
    g2 = (g - g1.astype(F32)).astype(BF16)
    ex_ref[...] = _dot(mat_ref[...], jnp.concatenate([g1, g2], axis=0))

    lvl = lvl_ref[...]
    zero = jnp.zeros((C, HGRN_DK), BF16)
    wide = 2 * HGRN_DK

    def block_diag(x):
        return jnp.concatenate([jnp.concatenate([x[:, :HGRN_DK], zero], axis=1),
                                jnp.concatenate([zero, x[:, HGRN_DK:]], axis=1)], axis=0)

    for hp in range(HGRN_HEADS // 2):
        sl = slice(hp * wide, (hp + 1) * wide)
        q = _silu(q_ref[:, sl])
        kk = kk_ref[:, sl]
        qb = q.astype(BF16)
        kb = kk.astype(BF16)
        attn = jnp.where(lvl == L, _dot_nt(qb, block_diag(kb)), 0.0)
        for l in range(L):
            xl = jnp.exp(ex_ref[(2 + l) * C:(3 + l) * C, sl]).astype(BF16)
            p = _dot_nt(qb * xl, block_diag(kb * xl))
            attn = jnp.where(lvl == l, p, attn)
        v2 = v_ref[:, sl]
        e_b = ex_ref[0:C, sl]
        e_last = ex_ref[C:2 * C, sl]
        b_last = e_b[0:1] + e_last[0:1]
        st = [state_ref[2 * hp + d] for d in range(2)]
        st_bd = jnp.concatenate([jnp.concatenate([st[0].astype(BF16), zero], axis=1),
                                 jnp.concatenate([zero, st[1].astype(BF16)], axis=1)], axis=0)
        o = _dot(attn.astype(BF16), block_diag(v2.astype(BF16)))
        o_ref[:, sl] = o + _dot_nt((q * jnp.exp(e_b)).astype(BF16), st_bd)
        kd = (kk * jnp.exp(e_last)).astype(BF16)
        for d in range(2):
            hs = slice(d * HGRN_DK, (d + 1) * HGRN_DK)
            vt = v2[:, hs].T.astype(BF16)
            state_ref[2 * hp + d] = st[d] * jnp.exp(b_last[:, hs]) + _dot(vt, kd[:, hs])


def hgrn_scan(proj, lb, batch, seq):
    T = proj.shape[0]
    C = SCAN_CHUNK
    nc = seq // C
    hk = HGRN_HEADS * HGRN_DK
    mats, lvl = _scan_tables()

    def rowblk(b, d, c):
        return b * nc + c + d * (nc - 1 - 2 * c)

    return pl.pallas_call(
        _hgrn_scan_kernel,
        grid=(batch, 2, nc),
        in_specs=[
            pl.BlockSpec((C, hk), lambda b, d, c: (rowblk(b, d, c), 0)),
            pl.BlockSpec((C, hk), lambda b, d, c: (rowblk(b, d, c), 1 + d)),
            pl.BlockSpec((C, hk), lambda b, d, c: (rowblk(b, d, c), 3)),
            pl.BlockSpec((None, 1, hk), lambda b, d, c: (d, 0, 0)),
            pl.BlockSpec((None,) + mats.shape[1:], lambda b, d, c: (d, 0, 0)),
            pl.BlockSpec((None, C, 2 * C), lambda b, d, c: (d, 0, 0)),
        ],
        out_specs=pl.BlockSpec((None, C, hk), lambda b, d, c: (d, rowblk(b, d, c), 0)),
        out_shape=jax.ShapeDtypeStruct((2, T, hk), F32),
        scratch_shapes=[pltpu.VMEM((HGRN_HEADS, HGRN_DV, HGRN_DK), F32), pltpu.VMEM((C, hk), F32),
                        pltpu.VMEM(((2 + SCAN_LEVELS) * C, hk), F32)],
        compiler_params=_params("parallel", "arbitrary", "arbitrary"),
        name="hgrn_scan",
    )(proj, proj, proj, lb, mats, lvl)


def _hgrn_out_kernel(o_ref, g_ref, x_ref, nw_ref, w_ref, lg_ref, lbias_ref, y_ref, yb_ref, a_ref, *, alpha):
    nw = nw_ref[...]
    for h in range(HGRN_HEADS):
        sl = slice(h * HGRN_DV, (h + 1) * HGRN_DV)
        o = o_ref[0, :, sl] + o_ref[1, :, sl]
        on = o * lax.rsqrt(jnp.mean(o * o, axis=-1, keepdims=True) + RMS_EPS) * nw
        a_ref[:, sl] = (on * _silu(g_ref[:, sl])).astype(BF16)
    hmix = _dot(a_ref[...], w_ref[...])
    y = _layer_norm(alpha * x_ref[...] + hmix, lg_ref[...], lbias_ref[...])
    y_ref[...] = y
    yb_ref[...] = y.astype(BF16)


def hgrn_out(o2, proj, x2d, nw, w, g, b, alpha, tm=256):
    T = x2d.shape[0]
    tm = math.gcd(T, tm)
    hv = HGRN_HEADS * HGRN_DV
    row = lambda i: (i, 0)
    return pl.pallas_call(
        functools.partial(_hgrn_out_kernel, alpha=alpha),
        grid=(T // tm,),
        in_specs=[pl.BlockSpec((2, tm, hv), lambda i: (0, i, 0)),
                  pl.BlockSpec((tm, hv), lambda i: (i, 4)),
                  pl.BlockSpec((tm, D_MODEL), row),
                  _full(nw.shape), _full(w.shape), _full(g.shape), _full(b.shape)],
        out_specs=[pl.BlockSpec((tm, D_MODEL), row), pl.BlockSpec((tm, D_MODEL), row)],
        out_shape=[jax.ShapeDtypeStruct((T, D_MODEL), F32), jax.ShapeDtypeStruct((T, D_MODEL), BF16)],
        scratch_shapes=[pltpu.VMEM((tm, hv), BF16)],
        compiler_params=_params("parallel"),
        name="hgrn_out",
    )(o2, proj, x2d, nw, w, g, b)


_PAIRS = [(a, b) for a in range(PEER_TOPK) for b in range(PEER_TOPK) if (a + 1) * (b + 1) <= PEER_TOPK]
_NPAIR_PAD = -(-len(_PAIRS) // 8) * 8


def _top_rows(s, k, payload=None):
    n = s.shape[0]
    iota = lax.broadcasted_iota(jnp.int32, s.shape, 0)
    sub = 8
    iota8 = lax.broadcasted_iota(jnp.int32, (sub, s.shape[1]), 0)
    vals, idxs = [], []
    for _ in range(k):
        nodes = [(s[r:r + sub], iota8 + r) for r in range(0, n, sub)]
        while len(nodes) > 1:
            nxt = [(jnp.maximum(va, vb), jnp.where(va >= vb, ia, ib))
                   for (va, ia), (vb, ib) in zip(nodes[0::2], nodes[1::2])]
            nodes = nxt + ([nodes[-1]] if len(nodes) % 2 else [])
        v8, i8 = nodes[0]
        m = jnp.max(v8, axis=0, keepdims=True)
        im = jnp.min(jnp.where(v8 == m, i8, n), axis=0, keepdims=True)
        hit = iota == im
        vals.append(m)
        if payload is None:
            idxs.append(im)
        else:
            idxs.append(jnp.max(jnp.where(hit, payload, -1), axis=0, keepdims=True))
        s = jnp.where(hit, -jnp.inf, s)
    return vals, idxs


def _retrieve_head(q_halves, keys_ref):
    sv, si = [], []
    for c in range(2):
        s = _dot_nt(keys_ref[c], q_halves[c])
        v_, i_ = _top_rows(s, PEER_TOPK)
        sv.append(v_)
        si.append(i_)
    return _combine_halves(sv, si)


def _combine_halves(sv, si):
    tm = sv[0][0].shape[1]
    neg = jnp.full((_NPAIR_PAD - len(_PAIRS), tm), -jnp.inf, F32)
    zero = jnp.zeros((_NPAIR_PAD - len(_PAIRS), tm), jnp.int32)
    cand = jnp.concatenate([sv[0][a] + sv[1][b] for a, b in _PAIRS] + [neg], axis=0)
    cid = jnp.concatenate([si[0][a] * N_KEYS + si[1][b] for a, b in _PAIRS] + [zero], axis=0)
    tv, te = _top_rows(cand, PEER_TOPK, payload=cid)
    tv = jnp.concatenate(tv, axis=0)
    ex = jnp.exp(tv - tv[0:1])
    return ex / jnp.sum(ex, axis=0, keepdims=True), jnp.concatenate(te, axis=0)


def _peer_topk_kernel(xb_ref, wq_ref, keys_ref, gate_ref, eid_ref, q_ref):
    q_ref[...] = _dot(xb_ref[...], wq_ref[...]).astype(BF16)
    half = PEER_DK // 2

    def head(h, carry):
        off = pl.multiple_of(h * PEER_DK, PEER_DK)
        gate, eid = _retrieve_head([q_ref[:, pl.ds(off + c * half, half)] for c in range(2)], keys_ref)
        row = pl.multiple_of(h * PEER_TOPK, PEER_TOPK)
        gate_ref[pl.ds(row, PEER_TOPK), :] = gate
        eid_ref[pl.ds(row, PEER_TOPK), :] = eid
        return carry

    lax.fori_loop(0, PEER_HEADS, head, 0)


def peer_topk(xb, wq, keys, tm=256):
    T = xb.shape[0]
    tm = math.gcd(T, tm)
    ns = PEER_HEADS * PEER_TOPK
    return pl.pallas_call(
        _peer_topk_kernel,
        grid=(T // tm,),
        in_specs=[pl.BlockSpec((tm, D_MODEL), lambda i: (i, 0)), _full(wq.shape), _full(keys.shape)],
        out_specs=[pl.BlockSpec((ns, tm), lambda i: (0, i)), pl.BlockSpec((ns, tm), lambda i: (0, i))],
        out_shape=[jax.ShapeDtypeStruct((ns, T), F32), jax.ShapeDtypeStruct((ns, T), jnp.int32)],
        scratch_shapes=[pltpu.VMEM((tm, PEER_HEADS * PEER_DK), BF16)],
        compiler_params=_params("parallel"),
        name="peer_topk",
    )(xb, wq, keys)


def _selection_rows(gate_ref, eid_ref, gt_ref, it_ref, jt_ref):
    eid_t = eid_ref[...].T
    gt_ref[...] = gate_ref[...].T
    it_ref[...] = lax.shift_right_logical(eid_t, int(math.log2(N_KEYS)))
    jt_ref[...] = lax.bitwise_and(eid_t, N_KEYS - 1)


def _gate_tiles(gt_ref, it_ref, jt_ref, tile_ref, sel0, tile0):
    ns = gt_ref.shape[1]
    iota = lax.broadcasted_iota(jnp.int32, (N_KEYS, ns), 0)
    iota_pad = lax.broadcasted_iota(jnp.int32, (N_KEYS + W_LHS_PAD, ns), 0)
    zero = jnp.zeros((N_KEYS, ns), BF16)
    g8 = gt_ref[pl.ds(sel0, W_GROUP), :]
    i8 = it_ref[pl.ds(sel0, W_GROUP), :]
    j8 = jt_ref[pl.ds(sel0, W_GROUP), :]
    for u in range(0, W_GROUP, 2):
        at0 = jnp.where(iota_pad == i8[u:u + 1], g8[u:u + 1], 0.0).astype(BF16)
        at1 = jnp.where(iota_pad == i8[u + 1:u + 2] + W_PAD, g8[u + 1:u + 2], 0.0).astype(BF16)
        bt = [jnp.where(iota == j8[u + d:u + d + 1], 1.0, 0.0).astype(BF16) for d in range(2)]
        lhs = jnp.concatenate([at0, at1], axis=1)
        rhs = jnp.concatenate([jnp.concatenate([bt[0], zero], axis=1),
                               jnp.concatenate([zero, bt[1]], axis=1)], axis=0)
        res = _dot_nt(lhs, rhs)
        row = pl.multiple_of((tile0 + u) * W_PITCH, 8)
        tile_ref[pl.ds(row, N_KEYS), :] = res[:N_KEYS, :N_KEYS]
        tile_ref[pl.ds(row + N_KEYS, N_KEYS + 2 * W_PAD), :] = res[:N_KEYS + 2 * W_PAD, N_KEYS:]


def _tile_rows_out(tile_ref, w_ref, ntok, i):
    col = tile_ref[pl.ds(i, ntok, stride=W_PITCH), :]
    lane0 = i * N_KEYS
    if not isinstance(i, int):
        lane0 = pl.multiple_of(lane0, N_KEYS)
    w_ref[:, pl.ds(lane0, N_KEYS)] = col.astype(BF16)


def _peer_w_kernel(gate_ref, eid_ref, w_ref, gt_ref, it_ref, jt_ref, tile_ref):
    tm = gate_ref.shape[1]
    _selection_rows(gate_ref, eid_ref, gt_ref, it_ref, jt_ref)

    def token_group(p, carry):
        base = pl.multiple_of(p * W_GROUP, W_GROUP)
        _gate_tiles(gt_ref, it_ref, jt_ref, tile_ref, base, base)
        return carry

    lax.fori_loop(0, tm // W_GROUP, token_group, 0)

    def key_rows(i2, carry):
        for d in range(W_ROWS):
            _tile_rows_out(tile_ref, w_ref, tm, i2 * W_ROWS + d)
        return carry

    lax.fori_loop(0, N_KEYS // W_ROWS, key_rows, 0)


def peer_w(gate, eid, tm=128):
    ns, T = gate.shape
    tm = math.gcd(T, tm)
    return pl.pallas_call(
        _peer_w_kernel,
        grid=(T // tm,),
        in_specs=[pl.BlockSpec((ns, tm), lambda i: (0, i)), pl.BlockSpec((ns, tm), lambda i: (0, i))],
        out_specs=pl.BlockSpec((tm, N_EXPERTS), lambda i: (i, 0)),
        out_shape=jax.ShapeDtypeStruct((T, N_EXPERTS), BF16),
        scratch_shapes=[pltpu.VMEM((tm, ns), F32), pltpu.VMEM((tm, ns), jnp.int32),
                        pltpu.VMEM((tm, ns), jnp.int32), pltpu.VMEM((tm * W_PITCH, N_KEYS), F32)],
        compiler_params=_params("parallel"),
        name="peer_gate_matrix",
    )(gate, eid)


def _dense_init(o_ref):
    @pl.when(pl.program_id(1) == 0)
    def _():
        o_ref[...] = jnp.zeros_like(o_ref)


def _dense_step(xb_ref, ut_ref, v_ref, w_ref, o_ref):
    h = _dot(xb_ref[...], ut_ref[...])
    act = 0.5 * h * (1.0 + lax.erf(h * (1.0 / math.sqrt(2.0))))
    hw = act * w_ref[...].astype(F32)
    o_ref[...] += _dot(hw.astype(BF16), v_ref[...])
    return hw


def _after(x, marker):
    z = pltpu.bitcast(marker, jnp.uint32)
    z = lax.shift_right_logical(lax.shift_right_logical(z, jnp.uint32(31)), jnp.uint32(1))
    z = pltpu.bitcast(z, F32)
    return x + jnp.tile(z, (x.shape[0] // z.shape[0], x.shape[1] // z.shape[1]))


def _peer_dense_kernel(xb_ref, ut_ref, v_ref, w_ref, o_ref):
    _dense_init(o_ref)
    _dense_step(xb_ref, ut_ref, v_ref, w_ref, o_ref)


def _peer_fused_kernel(*refs, with_gate, with_topk):
    refs = list(refs)
    xb_ref, ut_ref, v_ref, w_ref = refs[:4]
    del refs[:4]
    if with_gate:
        sg_ref, se_ref = refs[:2]
        del refs[:2]
    if with_topk:
        xn_ref, wq_ref, keys_ref = refs[:3]
        del refs[:3]
    o_ref = refs.pop(0)
    if with_gate:
        wn_ref = refs.pop(0)
    if with_topk:
        gate_ref, eid_ref = refs[:2]
        del refs[:2]
    if with_gate:
        gt_ref, it_ref, jt_ref, tile_ref = refs
    j = pl.program_id(1)

    @pl.when(j == 0)
    def _():
        o_ref[...] = jnp.zeros_like(o_ref)
        if with_gate:
            _selection_rows(sg_ref, se_ref, gt_ref, it_ref, jt_ref)

    if with_topk:
        half = PEER_DK // 2
        q = _dot(xn_ref[...], wq_ref[...]).astype(BF16)
        scores = [_dot_nt(keys_ref[c], q[:, c * half:(c + 1) * half]) for c in range(2)]
    hw = _dense_step(xb_ref, ut_ref, v_ref, w_ref, o_ref)
    if with_gate:
        ntok = wn_ref.shape[0]
        sel0 = pl.multiple_of(j * ntok, ntok)
        for g in range(0, ntok, W_GROUP):
            _gate_tiles(gt_ref, it_ref, jt_ref, tile_ref, sel0 + g, g)
        for i in range(N_KEYS):
            _tile_rows_out(tile_ref, wn_ref, ntok, i)
    if with_topk:
        marker = hw[hw.shape[0] - 8:, :LANES]
        row = pl.multiple_of(j * PEER_TOPK, PEER_TOPK)
        tm = xn_ref.shape[0]
        sub = math.gcd(tm, FUSE_LANES)
        for t0 in range(0, tm, sub):
            top = [_top_rows(_after(scores[c][:, t0:t0 + sub], marker), PEER_TOPK) for c in range(2)]
            gate, eid = _combine_halves([top[0][0], top[1][0]], [top[0][1], top[1][1]])
            gate_ref[pl.ds(row, PEER_TOPK), t0:t0 + sub] = gate
            eid_ref[pl.ds(row, PEER_TOPK), t0:t0 + sub] = eid
            marker = gate[:8]


def _dense_specs(tm, ne):
    return [pl.BlockSpec((tm, D_MODEL), lambda i, j: (i, 0)),
            pl.BlockSpec((D_MODEL, ne), lambda i, j: (0, j)),
            pl.BlockSpec((ne, D_MODEL), lambda i, j: (j, 0)),
            pl.BlockSpec((tm, ne), lambda i, j: (i, j))]


def peer_dense(xb, ut, v, wmat, tm=512, ne=N_EXPERTS // PEER_HEADS):
    T = xb.shape[0]
    tm = math.gcd(T, tm)
    return pl.pallas_call(
        _peer_dense_kernel,
        grid=(T // tm, N_EXPERTS // ne),
        in_specs=_dense_specs(tm, ne),
        out_specs=pl.BlockSpec((tm, D_MODEL), lambda i, j: (i, 0)),
        out_shape=jax.ShapeDtypeStruct((T, D_MODEL), F32),
        compiler_params=_params("parallel", "arbitrary"),
        name="peer_dense",
    )(xb, ut, v, wmat)


def peer_fused(xb, ut, v, wmat, sel_next=None, xn=None, wq=None, keys=None, tm=512):
    T = xb.shape[0]
    tm = math.gcd(T, tm)
    ne = N_EXPERTS // PEER_HEADS
    ns = PEER_HEADS * PEER_TOPK
    with_gate, with_topk = sel_next is not None, xn is not None
    ntok = tm // PEER_HEADS
    tile = lambda i, j: (i, 0)
    sel = lambda i, j: (0, i)
    args = [xb, ut, v, wmat]
    in_specs = _dense_specs(tm, ne)
    out_specs = [pl.BlockSpec((tm, D_MODEL), tile)]
    out_shape = [jax.ShapeDtypeStruct((T, D_MODEL), F32)]
    scratch = []
    if with_gate:
        assert sel_next[0].shape == (ns, T) and ntok % W_GROUP == 0
        args += list(sel_next)
        in_specs += [pl.BlockSpec((ns, tm), sel), pl.BlockSpec((ns, tm), sel)]
        out_specs.append(pl.BlockSpec((ntok, N_EXPERTS), lambda i, j: (i * PEER_HEADS + j, 0)))
        out_shape.append(jax.ShapeDtypeStruct((T, N_EXPERTS), BF16))
        scratch = [pltpu.VMEM((tm, ns), F32), pltpu.VMEM((tm, ns), jnp.int32), pltpu.VMEM((tm, ns), jnp.int32),
                   pltpu.VMEM((ntok * W_PITCH, N_KEYS), F32)]
    if with_topk:
        assert xn.shape[0] == T
        args += [xn, wq, keys]
        in_specs += [pl.BlockSpec((tm, D_MODEL), tile), pl.BlockSpec((D_MODEL, PEER_DK), lambda i, j: (0, j)),
                     _full(keys.shape)]
        out_specs += [pl.BlockSpec((ns, tm), sel), pl.BlockSpec((ns, tm), sel)]
        out_shape += [jax.ShapeDtypeStruct((ns, T), F32), jax.ShapeDtypeStruct((ns, T), jnp.int32)]
    return pl.pallas_call(
        functools.partial(_peer_fused_kernel, with_gate=with_gate, with_topk=with_topk),
        grid=(T // tm, PEER_HEADS),
        in_specs=in_specs,
        out_specs=out_specs,
        out_shape=out_shape,
        scratch_shapes=scratch,
        compiler_params=_params("parallel", "arbitrary"),
        name="peer_fused",
    )(*args)


def _ln_ple_kernel(x_ref, f_ref, p_ref, g_ref, b_ref, wg_ref, bg_ref, wp_ref, o_ref, *, alpha):
    y = _layer_norm(alpha * x_ref[...] + f_ref[...], g_ref[...], b_ref[...])
    gate = 1.0 / (1.0 + jnp.exp(-(_dot(y.astype(BF16), wg_ref[...]) + bg_ref[...])))
    o_ref[...] = y + gate * _dot(p_ref[...].astype(BF16), wp_ref[...])


def ln_ple(x2d, f2d, p2d, p_row0, g, b, wg, bg, wp, alpha, tm=512):
    T = x2d.shape[0]
    tm = math.gcd(math.gcd(T, tm), p_row0) if p_row0 else math.gcd(T, tm)
    blk0 = p_row0 // tm
    row = lambda i: (i, 0)
    return pl.pallas_call(
        functools.partial(_ln_ple_kernel, alpha=alpha),
        grid=(T // tm,),
        in_specs=[pl.BlockSpec((tm, D_MODEL), row), pl.BlockSpec((tm, D_MODEL), row),
                  pl.BlockSpec((tm, PLE_DIM), lambda i: (i + blk0, 0)),
                  _full(g.shape), _full(b.shape), _full(wg.shape), _full(bg.shape), _full(wp.shape)],
        out_specs=pl.BlockSpec((tm, D_MODEL), row),
        out_shape=jax.ShapeDtypeStruct((T, D_MODEL), F32),
        compiler_params=_params("parallel"),
        name="ln_ple",
    )(x2d, f2d, p2d, g, b, wg, bg, wp)


def _rot_cols(w):
    half = MLA_ROPE // 2
    return jnp.concatenate([-w[..., half:], w[..., :half]], axis=-1)


def _mla_weights(w_a, q_norm, kv_norm, w_uq, w_ukv):
    zk = jnp.zeros((D_MODEL, LANES - MLA_ROPE), F32)
    w_kr = w_a[:, MLA_Q_LORA + MLA_KV_LORA:]
    wa = jnp.concatenate([w_a[:, :MLA_Q_LORA + MLA_KV_LORA], w_kr, zk, _rot_cols(w_kr), zk], axis=1)
    uq = w_uq.reshape(MLA_Q_LORA, MLA_HEADS, MLA_NOPE + MLA_ROPE)
    zq = jnp.zeros((MLA_Q_LORA, MLA_HEADS, MLA_QK_PAD - MLA_NOPE - MLA_ROPE), F32)
    wuq = jnp.concatenate([uq, zq], axis=-1).reshape(MLA_Q_LORA, MLA_HEADS * MLA_QK_PAD)
    wuqr = jnp.concatenate([jnp.zeros_like(uq[..., :MLA_NOPE]), _rot_cols(uq[..., MLA_NOPE:]), zq],
                           axis=-1).reshape(MLA_Q_LORA, MLA_HEADS * MLA_QK_PAD)
    ukv = w_ukv.reshape(MLA_KV_LORA, MLA_HEADS, MLA_NOPE + MLA_V)
    return {
        "wa": wa.astype(BF16),
        "qn": q_norm.astype(F32).reshape(1, -1),
        "kvn": kv_norm.astype(F32).reshape(1, -1),
        "wuq": wuq.astype(BF16),
        "wuqr": wuqr.astype(BF16),
        "wuk": ukv[..., :MLA_NOPE].reshape(MLA_KV_LORA, -1).astype(BF16),
        "wuv": ukv[..., MLA_NOPE:].reshape(MLA_KV_LORA, -1).astype(BF16),
    }


def _rope_tabs(seq):
    inv_freq = 1.0 / (ROPE_THETA ** (jnp.arange(0, MLA_ROPE, 2, dtype=F32) / MLA_ROPE))
    ang = jnp.arange(seq, dtype=F32)[:, None] * inv_freq[None, :]
    emb = jnp.concatenate([ang, ang], axis=-1)
    cos, sin = jnp.cos(emb), jnp.sin(emb)
    one = jnp.ones((seq, MLA_NOPE), F32)
    z64 = jnp.zeros((seq, LANES - MLA_ROPE), F32)
    z128 = jnp.zeros((seq, MLA_NOPE), F32)
    return (jnp.concatenate([one, cos, z64], axis=1), jnp.concatenate([z128, sin, z64], axis=1),
            jnp.concatenate([cos, z64], axis=1), jnp.concatenate([sin, z64], axis=1))


def _mixer(i, x2d, batch, seq, lbs, mla_w, hgrn_w, ln_w, alpha, tabs):
    j = i // 2
    g1, b1, _, _ = ln_w[i]
    if i % 2 == 0:
        w = mla_w[j]
        q2d, k2d, v2d = mla_proj(x2d, seq, w, tabs)
        o2d = attention(q2d, k2d.T, v2d, batch, seq)
        return proj_res_ln(o2d, x2d, w["wo"], g1, b1, alpha)
    w = hgrn_w[j]
    proj = matmul(x2d, w["win"])
    o2 = hgrn_scan(proj, lbs[:, j], batch, seq)
    return hgrn_out(o2, proj, x2d, w["nw"], w["wo"], g1, b1, alpha)


def _trunk(groups, lbs, mla_w, hgrn_w, peer_w_, ln_w, ple_w, depth):
    alpha = (2 * depth) ** 0.25
    n = len(groups)
    total = depth * n
    tabs = {s: _rope_tabs(s) for s in {g["seq"] for g in groups}}
    xin = {k: groups[k]["x"] for k in range(n)}
    memo = {}

    def mixed(m):
        if m not in memo:
            g = groups[m % n]
            memo[m] = _mixer(m // n, xin[m], g["batch"], g["seq"], lbs, mla_w, hgrn_w, ln_w, alpha, tabs[g["seq"]])
        return memo[m]

    def finish(m, f):
        i, k = divmod(m, n)
        g = groups[k]
        _, _, g2, b2 = ln_w[i]
        wg, bg, wp = ple_w[i]
        return ln_ple(mixed(m)[0], f, g["p"][i], g["row0"], g2, b2, wg, bg, wp, alpha)

    pw = lambda m: peer_w_[m // n]
    pipelined = n > 2 and all(g["x"].shape[0] == groups[0]["x"].shape[0] for g in groups)
    sel, wmat = {}, {}
    if pipelined:
        for m in range(2):
            sel[m] = peer_topk(mixed(m)[1], pw(m)["wq"], pw(m)["keys"])
        wmat[0] = peer_w(*sel[0])
    for m in range(total):
        x1b = mixed(m)[1]
        if pipelined:
            sel_next = sel.pop(m + 1) if m + 1 < total else None
            xn = mixed(m + 2)[1] if m + 2 < total else None
            pn = pw(min(m + 2, total - 1))
            res = list(peer_fused(x1b, pw(m)["ut"], pw(m)["v"], wmat.pop(m), sel_next, xn, pn["wq"], pn["keys"]))
            f = res.pop(0)
            if sel_next is not None:
                wmat[m + 1] = res.pop(0)
            if xn is not None:
                sel[m + 2] = (res[0], res[1])
        else:
            f = peer_dense(x1b, pw(m)["ut"], pw(m)["v"], peer_w(*peer_topk(x1b, pw(m)["wq"], pw(m)["keys"])))
        xin[m + n] = finish(m, f)
        del memo[m], xin[m]
    return [xin[total + k] for k in range(n)]


def _token_groups(x, p, target):
    batch, seq, _ = x.shape
    depth = p.shape[0]
    T = batch * seq
    x2d = x.reshape(T, D_MODEL)
    p2d = p.reshape(depth, T, PLE_DIM)
    n = T // target if (target % seq == 0 and T % target == 0) else 1
    tg = T // n
    return [{"x": x2d if n == 1 else x2d[k * tg:(k + 1) * tg], "p": p2d, "row0": k * tg,
             "batch": batch // n, "seq": seq} for k in range(n)]


def kernel(x_prompt, x_sample, p_prompt, p_sample, mla_w_a, mla_q_norm, mla_kv_norm, mla_w_uq, mla_w_ukv, mla_w_o, hgrn_w_in, hgrn_lb, hgrn_norm, hgrn_w_o, peer_w_q, peer_sub_keys, peer_u, peer_v, ln1_g, ln1_b, ln2_g, ln2_b, ple_gate_w, ple_gate_b, ple_proj):
    depth = peer_w_q.shape[0]
    lbs = lower_bounds(hgrn_lb)
    mla_w = []
    for j in range(mla_w_a.shape[0]):
        w = _mla_weights(mla_w_a[j], mla_q_norm[j], mla_kv_norm[j], mla_w_uq[j], mla_w_ukv[j])
        w["wo"] = mla_w_o[j].astype(BF16)
        mla_w.append(w)
    hgrn_w = [{"win": hgrn_w_in[j].astype(BF16), "nw": hgrn_norm[j].astype(F32).reshape(1, -1),
               "wo": hgrn_w_o[j].astype(BF16)} for j in range(hgrn_w_in.shape[0])]
    peer_w_ = [{"wq": peer_w_q[i].astype(BF16), "keys": peer_sub_keys[i].astype(BF16),
                "ut": peer_u[i].astype(BF16).T, "v": peer_v[i].astype(BF16)} for i in range(depth)]
    r = lambda a: a.astype(F32).reshape(1, -1)
    ln_w = [(r(ln1_g[i]), r(ln1_b[i]), r(ln2_g[i]), r(ln2_b[i])) for i in range(depth)]
    ple_w = [(ple_gate_w[i].astype(BF16), r(ple_gate_b[i]), ple_proj[i].astype(BF16)) for i in range(depth)]
    target = max(x_prompt.shape[1], x_sample.shape[1])
    g_prompt = _token_groups(x_prompt, p_prompt, target)
    groups = g_prompt + _token_groups(x_sample, p_sample, target)
    ys = _trunk(groups, lbs, mla_w, hgrn_w, peer_w_, ln_w, ple_w, depth)
    join = lambda parts: parts[0] if len(parts) == 1 else jnp.concatenate(parts, axis=0)
    y_prompt = join(ys[:len(g_prompt)]).reshape(x_prompt.shape)
    y_sample = join(ys[len(g_prompt):]).reshape(x_sample.shape)
    return (y_prompt, y_sample)
```

```python
import functools
import math

import numpy as np
import jax
import jax.numpy as jnp
from jax import lax
from jax.experimental import pallas as pl
from jax.experimental.pallas import tpu as pltpu

D_MODEL = 1024
PLE_DIM = 256
MLA_HEADS = 8
MLA_NOPE = 128
MLA_ROPE = 64
MLA_V = 128
MLA_Q_LORA = 384
MLA_KV_LORA = 256
MLA_SCALE = (MLA_NOPE + MLA_ROPE) ** -0.5
MLA_QK_PAD = 256
LOG2E = math.log2(math.e)
ROPE_THETA = 10000.0
HGRN_HEADS = 8
HGRN_DK = 128
HGRN_DV = 128
GATE_FLOOR = 1e-30
PEER_HEADS = 8
PEER_DK = 256
N_KEYS = 128
N_EXPERTS = N_KEYS * N_KEYS
PEER_TOPK = 16
LN_EPS = 1e-5
RMS_EPS = 1e-6

LANES = 128
VMEM_LIMIT = 56 * 1024 * 1024

SCAN_CHUNK = 128
SCAN_LEVELS = 7
W_PAD = 4
W_PITCH = N_KEYS + W_PAD
W_LHS_PAD = 16
W_GROUP = 32
W_ROWS = 8
FUSE_LANES = 256

F32 = jnp.float32
BF16 = jnp.bfloat16

_NT = (((1,), (1,)), ((), ()))


def _params(*sem):
    return pltpu.CompilerParams(dimension_semantics=sem, vmem_limit_bytes=VMEM_LIMIT)


def _dot(a, b):
    return jnp.dot(a, b, preferred_element_type=F32)


def _dot_nt(a, b):
    return lax.dot_general(a, b, _NT, preferred_element_type=F32)


def _layer_norm(y, g, b):
    mu = jnp.mean(y, axis=-1, keepdims=True)
    yc = y - mu
    var = jnp.mean(yc * yc, axis=-1, keepdims=True)
    return yc * lax.rsqrt(var + LN_EPS) * g + b


def _silu(t):
    return t * (1.0 / (1.0 + jnp.exp(-t)))


def _full(shape):
    nd = len(shape)
    return pl.BlockSpec(shape, lambda *_: (0,) * nd)


def _lower_bounds_kernel(lb_ref, out_ref):
    n = lb_ref.shape[1]
    for d in range(lb_ref.shape[0]):
        rows = [lb_ref[d, j] for j in range(n)]
        m = rows[0]
        for r in rows[1:]:
            m = jnp.maximum(m, r)
        ex = [jnp.exp(r - m) for r in rows]
        tot = ex[0]
        for e in ex[1:]:
            tot = tot + e
        sm = [e / tot for e in ex]
        cum = sm[0]
        out_ref[d, 0] = cum - sm[0]
        for j in range(1, n):
            cum = cum + sm[j]
            out_ref[d, j] = cum - sm[0]


def lower_bounds(hgrn_lb):
    two, n, hk = hgrn_lb.shape
    lb4 = hgrn_lb.astype(F32).reshape(two, n, 1, hk)
    out = pl.pallas_call(
        _lower_bounds_kernel,
        out_shape=jax.ShapeDtypeStruct((two, n, 1, hk), F32),
        name="hgrn_lower_bounds",
    )(lb4)
    return out


def _mla_proj_kernel(x_ref, wa_ref, qn_ref, kvn_ref, wuq_ref, wuqr_ref, wuk_ref, wuv_ref,
                     cq_ref, sq_ref, ck_ref, sk_ref, q_ref, k_ref, v_ref):
    xb = x_ref[...].astype(BF16)
    a = _dot(xb, wa_ref[...])
    c_q = a[:, :MLA_Q_LORA]
    c_kv = a[:, MLA_Q_LORA:MLA_Q_LORA + MLA_KV_LORA]
    kr = a[:, 640:768]
    krr = a[:, 768:896]
    cqn = (c_q * lax.rsqrt(jnp.mean(c_q * c_q, axis=-1, keepdims=True) + RMS_EPS) * qn_ref[...]).astype(BF16)
    ckvn = (c_kv * lax.rsqrt(jnp.mean(c_kv * c_kv, axis=-1, keepdims=True) + RMS_EPS) * kvn_ref[...]).astype(BF16)
    q = _dot(cqn, wuq_ref[...])
    qr = _dot(cqn, wuqr_ref[...])
    cq = cq_ref[...]
    sq = sq_ref[...]
    kro = (kr * ck_ref[...] + krr * sk_ref[...]).astype(BF16)
    kn = _dot(ckvn, wuk_ref[...]).astype(BF16)
    v_ref[...] = _dot(ckvn, wuv_ref[...]).astype(BF16)
    for h in range(MLA_HEADS):
        lo = h * MLA_QK_PAD
        qh = (q[:, lo:lo + MLA_QK_PAD] * cq + qr[:, lo:lo + MLA_QK_PAD] * sq) * (MLA_SCALE * LOG2E)
        q_ref[:, lo:lo + MLA_QK_PAD] = qh.astype(BF16)
        k_ref[:, lo:lo + MLA_NOPE] = kn[:, h * MLA_NOPE:(h + 1) * MLA_NOPE]
        k_ref[:, lo + MLA_NOPE:lo + MLA_QK_PAD] = kro


def mla_proj(x2d, seq, w, tabs, tm=256):
    T = x2d.shape[0]
    tm = math.gcd(seq, tm)
    nper = seq // tm
    cq, sq, ck, sk = tabs
    row = lambda i: (i, 0)
    pos = lambda i: (i % nper, 0)
    hq = MLA_HEADS * MLA_QK_PAD
    return pl.pallas_call(
        _mla_proj_kernel,
        grid=(T // tm,),
        in_specs=[
            pl.BlockSpec((tm, D_MODEL), row),
            _full(w["wa"].shape), _full(w["qn"].shape), _full(w["kvn"].shape),
            _full(w["wuq"].shape), _full(w["wuqr"].shape), _full(w["wuk"].shape), _full(w["wuv"].shape),
            pl.BlockSpec((tm, MLA_QK_PAD), pos), pl.BlockSpec((tm, MLA_QK_PAD), pos),
            pl.BlockSpec((tm, LANES), pos), pl.BlockSpec((tm, LANES), pos),
        ],
        out_specs=[pl.BlockSpec((tm, hq), row), pl.BlockSpec((tm, hq), row),
                   pl.BlockSpec((tm, MLA_HEADS * MLA_V), row)],
        out_shape=[jax.ShapeDtypeStruct((T, hq), BF16), jax.ShapeDtypeStruct((T, hq), BF16),
                   jax.ShapeDtypeStruct((T, MLA_HEADS * MLA_V), BF16)],
        compiler_params=_params("parallel"),
        name="mla_proj",
    )(x2d, w["wa"], w["qn"], w["kvn"], w["wuq"], w["wuqr"], w["wuk"], w["wuv"], cq, sq, ck, sk)


def _attn_kernel(q_ref, kt_ref, v_ref, o_ref, *, tk, unroll):
    q = q_ref[...]
    tq = q.shape[0]
    nk = kt_ref.shape[1] // tk

    def body(c, carry):
        m, l, acc = carry
        for u in range(unroll):
            off = pl.multiple_of((c * unroll + u) * tk, tk)
            s = _dot(q, kt_ref[:, pl.ds(off, tk)])
            m_new = jnp.maximum(m, jnp.max(s, axis=-1, keepdims=True))
            corr = jnp.exp2(m - m_new)
            p = jnp.exp2(s - m_new)
            l = corr * l + jnp.sum(p, axis=-1, keepdims=True)
            acc = corr * acc + _dot(p.astype(BF16), v_ref[pl.ds(off, tk), :])
            m = m_new
        return m, l, acc

    init = (jnp.full((tq, 1), -jnp.inf, F32), jnp.zeros((tq, 1), F32), jnp.zeros((tq, MLA_V), F32))
    _, l, acc = lax.fori_loop(0, nk // unroll, body, init)
    o_ref[...] = (acc / l).astype(o_ref.dtype)


def attention(q2d, kt2d, v2d, batch, seq, tq=1024, tk=512, unroll=4):
    T = q2d.shape[0]
    tq, tk = math.gcd(seq, tq), math.gcd(seq, tk)
    unroll = math.gcd(seq // tk, unroll)
    nq = seq // tq
    return pl.pallas_call(
        functools.partial(_attn_kernel, tk=tk, unroll=unroll),
        grid=(batch, MLA_HEADS, nq),
        in_specs=[
            pl.BlockSpec((tq, MLA_QK_PAD), lambda b, h, i: (b * nq + i, h)),
            pl.BlockSpec((MLA_QK_PAD, seq), lambda b, h, i: (h, b)),
            pl.BlockSpec((seq, MLA_V), lambda b, h, i: (b, h)),
        ],
        out_specs=pl.BlockSpec((tq, MLA_V), lambda b, h, i: (b * nq + i, h)),
        out_shape=jax.ShapeDtypeStruct((T, MLA_HEADS * MLA_V), BF16),
        compiler_params=_params("parallel", "parallel", "arbitrary"),
        name="mla_attention",
    )(q2d, kt2d, v2d)


def _proj_res_ln_kernel(a_ref, x_ref, w_ref, g_ref, b_ref, o_ref, ob_ref, *, alpha):
    h = _dot(a_ref[...], w_ref[...])
    y = _layer_norm(alpha * x_ref[...] + h, g_ref[...], b_ref[...])
    o_ref[...] = y
    ob_ref[...] = y.astype(BF16)


def proj_res_ln(a2d, x2d, w, g, b, alpha, tm=512):
    T = x2d.shape[0]
    tm = math.gcd(T, tm)
    row = lambda i: (i, 0)
    return pl.pallas_call(
        functools.partial(_proj_res_ln_kernel, alpha=alpha),
        grid=(T // tm,),
        in_specs=[pl.BlockSpec((tm, a2d.shape[1]), row), pl.BlockSpec((tm, D_MODEL), row),
                  _full(w.shape), _full(g.shape), _full(b.shape)],
        out_specs=[pl.BlockSpec((tm, D_MODEL), row), pl.BlockSpec((tm, D_MODEL), row)],
        out_shape=[jax.ShapeDtypeStruct((T, D_MODEL), F32), jax.ShapeDtypeStruct((T, D_MODEL), BF16)],
        compiler_params=_params("parallel"),
        name="proj_res_ln",
    )(a2d, x2d, w, g, b)


def _scan_tables():
    C, L = SCAN_CHUNK, SCAN_LEVELS
    mats = np.zeros((2, (2 + L) * C, C), np.float32)
    lvl = np.full((2, C, C), -1, np.int32)
    for d in range(2):
        p = np.arange(C) if d == 0 else C - 1 - np.arange(C)
        pt, pu = p[:, None], p[None, :]
        mats[d, 0:C] = pu <= pt
        mats[d, C:2 * C] = pu > pt
        for l in range(L):
            m = 1 << l
            r = (pt // (2 * m)) * (2 * m) + m - 1
            qside = (pt % (2 * m)) >= m
            e = np.where(qside, (pu > r) & (pu <= pt), (pu > pt) & (pu <= r))
            mats[d, (2 + l) * C:(3 + l) * C] = e
            same = (pt // (2 * m)) == (pu // (2 * m))
            lvl[d][same & qside & ((pu % (2 * m)) < m)] = l
        lvl[d][pt == pu] = L
    return jnp.asarray(np.concatenate([mats, mats], axis=2), BF16), jnp.asarray(np.concatenate([lvl, lvl], axis=2))


def _hgrn_scan_kernel(x_ref, w_ref, lb_ref, mat_ref, lvl_ref, o_ref, state_ref, kk_ref, ex_ref, q_ref, v_ref):
    C, L = SCAN_CHUNK, SCAN_LEVELS
    hk = HGRN_HEADS * HGRN_DK

    @pl.when(pl.program_id(2) == 0)
    def _():
        state_ref[...] = jnp.zeros_like(state_ref)

    proj = _dot(x_ref[...].astype(BF16), w_ref[...])
    q_ref[...] = proj[:, :hk]
    v_ref[...] = proj[:, 2 * hk:]
    z = proj[:, hk:2 * hk]
    lb = lb_ref[...]
    e = jnp.exp(-jnp.abs(z))
    r = 1.0 / (1.0 + e)
    er = e * r
    pos = z >= 0
    f = lb + (1.0 - lb) * jnp.where(pos, r, er)
    g = jnp.log(jnp.maximum(f, GATE_FLOOR))
    kk_ref[...] = (1.0 - lb) * jnp.where(pos, er, r)
    g1 = g.astype(BF16)
    g2 = (g - g1.astype(F32)).astype(BF16)
    ex_ref[...] = _dot(mat_ref[...], jnp.concatenate([g1, g2], axis=0))

    lvl = lvl_ref[...]
    zero = jnp.zeros((C, HGRN_DK), BF16)
    wide = 2 * HGRN_DK

    def block_diag(x):
        return jnp.concatenate([jnp.concatenate([x[:, :HGRN_DK], zero], axis=1),
                                jnp.concatenate([zero, x[:, HGRN_DK:]], axis=1)], axis=0)

    for hp in range(HGRN_HEADS // 2):
        sl = slice(hp * wide, (hp + 1) * wide)
        q = _silu(q_ref[:, sl])
        kk = kk_ref[:, sl]
        qb = q.astype(BF16)
        kb = kk.astype(BF16)
        attn = jnp.where(lvl == L, _dot_nt(qb, block_diag(kb)), 0.0)
        for l in range(L):
            xl = jnp.exp(ex_ref[(2 + l) * C:(3 + l) * C, sl]).astype(BF16)
            p = _dot_nt(qb * xl, block_diag(kb * xl))
            attn = jnp.where(lvl == l, p, attn)
        v2 = v_ref[:, sl]
        e_b = ex_ref[0:C, sl]
        e_last = ex_ref[C:2 * C, sl]
        b_last = e_b[0:1] + e_last[0:1]
        st = [state_ref[2 * hp + d] for d in range(2)]
        st_bd = jnp.concatenate([jnp.concatenate([st[0].astype(BF16), zero], axis=1),
                                 jnp.concatenate([zero, st[1].astype(BF16)], axis=1)], axis=0)
        o = _dot(attn.astype(BF16), block_diag(v2.astype(BF16)))
        o_ref[:, sl] = o + _dot_nt((q * jnp.exp(e_b)).astype(BF16), st_bd)
        kd = (kk * jnp.exp(e_last)).astype(BF16)
        for d in range(2):
            hs = slice(d * HGRN_DK, (d + 1) * HGRN_DK)
            vt = v2[:, hs].T.astype(BF16)
            state_ref[2 * hp + d] = st[d] * jnp.exp(b_last[:, hs]) + _dot(vt, kd[:, hs])


def hgrn_scan(x2d, w_qzv, lb, batch, seq):
    T = x2d.shape[0]
    C = SCAN_CHUNK
    nc = seq // C
    hk = HGRN_HEADS * HGRN_DK
    mats, lvl = _scan_tables()

    def rowblk(b, d, c):
        return b * nc + c + d * (nc - 1 - 2 * c)

    return pl.pallas_call(
        _hgrn_scan_kernel,
        grid=(batch, 2, nc),
        in_specs=[
            pl.BlockSpec((C, D_MODEL), lambda b, d, c: (rowblk(b, d, c), 0)),
            pl.BlockSpec((None,) + w_qzv.shape[1:], lambda b, d, c: (d, 0, 0)),
            pl.BlockSpec((None, 1, hk), lambda b, d, c: (d, 0, 0)),
            pl.BlockSpec((None,) + mats.shape[1:], lambda b, d, c: (d, 0, 0)),
            pl.BlockSpec((None, C, 2 * C), lambda b, d, c: (d, 0, 0)),
        ],
        out_specs=pl.BlockSpec((None, C, hk), lambda b, d, c: (d, rowblk(b, d, c), 0)),
        out_shape=jax.ShapeDtypeStruct((2, T, hk), F32),
        scratch_shapes=[pltpu.VMEM((HGRN_HEADS, HGRN_DV, HGRN_DK), F32), pltpu.VMEM((C, hk), F32),
                        pltpu.VMEM(((2 + SCAN_LEVELS) * C, hk), F32),
                        pltpu.VMEM((C, hk), F32), pltpu.VMEM((C, hk), F32)],
        compiler_params=_params("parallel", "arbitrary", "arbitrary"),
        name="hgrn_scan",
    )(x2d, w_qzv, lb, mats, lvl)


def _hgrn_out_kernel(o_ref, wg_ref, x_ref, nw_ref, w_ref, lg_ref, lbias_ref, y_ref, yb_ref, a_ref, *, alpha):
    nw = nw_ref[...]
    gate = _silu(_dot(x_ref[...].astype(BF16), wg_ref[...]))
    for h in range(HGRN_HEADS):
        sl = slice(h * HGRN_DV, (h + 1) * HGRN_DV)
        o = o_ref[0, :, sl] + o_ref[1, :, sl]
        on = o * lax.rsqrt(jnp.mean(o * o, axis=-1, keepdims=True) + RMS_EPS) * nw
        a_ref[:, sl] = (on * gate[:, sl]).astype(BF16)
    hmix = _dot(a_ref[...], w_ref[...])
    y = _layer_norm(alpha * x_ref[...] + hmix, lg_ref[...], lbias_ref[...])
    y_ref[...] = y
    yb_ref[...] = y.astype(BF16)


def hgrn_out(o2, wgate, x2d, nw, w, g, b, alpha, tm=256):
    T = x2d.shape[0]
    tm = math.gcd(T, tm)
    hv = HGRN_HEADS * HGRN_DV
    row = lambda i: (i, 0)
    return pl.pallas_call(
        functools.partial(_hgrn_out_kernel, alpha=alpha),
        grid=(T // tm,),
        in_specs=[pl.BlockSpec((2, tm, hv), lambda i: (0, i, 0)),
                  _full(wgate.shape),
                  pl.BlockSpec((tm, D_MODEL), row),
                  _full(nw.shape), _full(w.shape), _full(g.shape), _full(b.shape)],
        out_specs=[pl.BlockSpec((tm, D_MODEL), row), pl.BlockSpec((tm, D_MODEL), row)],
        out_shape=[jax.ShapeDtypeStruct((T, D_MODEL), F32), jax.ShapeDtypeStruct((T, D_MODEL), BF16)],
        scratch_shapes=[pltpu.VMEM((tm, hv), BF16)],
        compiler_params=_params("parallel"),
        name="hgrn_out",
    )(o2, wgate, x2d, nw, w, g, b)


_PAIRS = [(a, b) for a in range(PEER_TOPK) for b in range(PEER_TOPK) if (a + 1) * (b + 1) <= PEER_TOPK]
_NPAIR_PAD = -(-len(_PAIRS) // 8) * 8


def _top_rows(s, k, payload=None):
    n = s.shape[0]
    iota = lax.broadcasted_iota(jnp.int32, s.shape, 0)
    sub = 8
    iota8 = lax.broadcasted_iota(jnp.int32, (sub, s.shape[1]), 0)
    vals, idxs = [], []
    for _ in range(k):
        nodes = [(s[r:r + sub], iota8 + r) for r in range(0, n, sub)]
        while len(nodes) > 1:
            nxt = [(jnp.maximum(va, vb), jnp.where(va >= vb, ia, ib))
                   for (va, ia), (vb, ib) in zip(nodes[0::2], nodes[1::2])]
            nodes = nxt + ([nodes[-1]] if len(nodes) % 2 else [])
        v8, i8 = nodes[0]
        m = jnp.max(v8, axis=0, keepdims=True)
        im = jnp.min(jnp.where(v8 == m, i8, n), axis=0, keepdims=True)
        hit = iota == im
        vals.append(m)
        if payload is None:
            idxs.append(im)
        else:
            idxs.append(jnp.max(jnp.where(hit, payload, -1), axis=0, keepdims=True))
        s = jnp.where(hit, -jnp.inf, s)
    return vals, idxs


def _retrieve_head(q_halves, keys_ref):
    sv, si = [], []
    for c in range(2):
        s = _dot_nt(keys_ref[c], q_halves[c])
        v_, i_ = _top_rows(s, PEER_TOPK)
        sv.append(v_)
        si.append(i_)
    return _combine_halves(sv, si)


def _combine_halves(sv, si):
    tm = sv[0][0].shape[1]
    neg = jnp.full((_NPAIR_PAD - len(_PAIRS), tm), -jnp.inf, F32)
    zero = jnp.zeros((_NPAIR_PAD - len(_PAIRS), tm), jnp.int32)
    cand = jnp.concatenate([sv[0][a] + sv[1][b] for a, b in _PAIRS] + [neg], axis=0)
    cid = jnp.concatenate([si[0][a] * N_KEYS + si[1][b] for a, b in _PAIRS] + [zero], axis=0)
    tv, te = _top_rows(cand, PEER_TOPK, payload=cid)
    tv = jnp.concatenate(tv, axis=0)
    ex = jnp.exp(tv - tv[0:1])
    return ex / jnp.sum(ex, axis=0, keepdims=True), jnp.concatenate(te, axis=0)


def _peer_topk_kernel(xb_ref, wq_ref, keys_ref, gate_ref, eid_ref, q_ref):
    q_ref[...] = _dot(xb_ref[...], wq_ref[...]).astype(BF16)
    half = PEER_DK // 2

    def head(h, carry):
        off = pl.multiple_of(h * PEER_DK, PEER_DK)
        gate, eid = _retrieve_head([q_ref[:, pl.ds(off + c * half, half)] for c in range(2)], keys_ref)
        row = pl.multiple_of(h * PEER_TOPK, PEER_TOPK)
        gate_ref[pl.ds(row, PEER_TOPK), :] = gate
        eid_ref[pl.ds(row, PEER_TOPK), :] = eid
        return carry

    lax.fori_loop(0, PEER_HEADS, head, 0)


def peer_topk(xb, wq, keys, tm=256):
    T = xb.shape[0]
    tm = math.gcd(T, tm)
    ns = PEER_HEADS * PEER_TOPK
    return pl.pallas_call(
        _peer_topk_kernel,
        grid=(T // tm,),
        in_specs=[pl.BlockSpec((tm, D_MODEL), lambda i: (i, 0)), _full(wq.shape), _full(keys.shape)],
        out_specs=[pl.BlockSpec((ns, tm), lambda i: (0, i)), pl.BlockSpec((ns, tm), lambda i: (0, i))],
        out_shape=[jax.ShapeDtypeStruct((ns, T), F32), jax.ShapeDtypeStruct((ns, T), jnp.int32)],
        scratch_shapes=[pltpu.VMEM((tm, PEER_HEADS * PEER_DK), BF16)],
        compiler_params=_params("parallel"),
        name="peer_topk",
    )(xb, wq, keys)


def _selection_rows(gate_ref, eid_ref, gt_ref, it_ref, jt_ref):
    eid_t = eid_ref[...].T
    gt_ref[...] = gate_ref[...].T
    it_ref[...] = lax.shift_right_logical(eid_t, int(math.log2(N_KEYS)))
    jt_ref[...] = lax.bitwise_and(eid_t, N_KEYS - 1)


def _gate_tiles(gt_ref, it_ref, jt_ref, tile_ref, sel0, tile0):
    ns = gt_ref.shape[1]
    iota = lax.broadcasted_iota(jnp.int32, (N_KEYS, ns), 0)
    iota_pad = lax.broadcasted_iota(jnp.int32, (N_KEYS + W_LHS_PAD, ns), 0)
    zero = jnp.zeros((N_KEYS, ns), BF16)
    g8 = gt_ref[pl.ds(sel0, W_GROUP), :]
    i8 = it_ref[pl.ds(sel0, W_GROUP), :]
    j8 = jt_ref[pl.ds(sel0, W_GROUP), :]
    for u in range(0, W_GROUP, 2):
        at0 = jnp.where(iota_pad == i8[u:u + 1], g8[u:u + 1], 0.0).astype(BF16)
        at1 = jnp.where(iota_pad == i8[u + 1:u + 2] + W_PAD, g8[u + 1:u + 2], 0.0).astype(BF16)
        bt = [jnp.where(iota == j8[u + d:u + d + 1], 1.0, 0.0).astype(BF16) for d in range(2)]
        lhs = jnp.concatenate([at0, at1], axis=1)
        rhs = jnp.concatenate([jnp.concatenate([bt[0], zero], axis=1),
                               jnp.concatenate([zero, bt[1]], axis=1)], axis=0)
        res = _dot_nt(lhs, rhs)
        row = pl.multiple_of((tile0 + u) * W_PITCH, 8)
        tile_ref[pl.ds(row, N_KEYS), :] = res[:N_KEYS, :N_KEYS]
        tile_ref[pl.ds(row + N_KEYS, N_KEYS + 2 * W_PAD), :] = res[:N_KEYS + 2 * W_PAD, N_KEYS:]


def _tile_rows_out(tile_ref, w_ref, ntok, i):
    col = tile_ref[pl.ds(i, ntok, stride=W_PITCH), :]
    lane0 = i * N_KEYS
    if not isinstance(i, int):
        lane0 = pl.multiple_of(lane0, N_KEYS)
    w_ref[:, pl.ds(lane0, N_KEYS)] = col.astype(BF16)


def _peer_w_kernel(gate_ref, eid_ref, w_ref, gt_ref, it_ref, jt_ref, tile_ref):
    tm = gate_ref.shape[1]
    _selection_rows(gate_ref, eid_ref, gt_ref, it_ref, jt_ref)

    def token_group(p, carry):
        base = pl.multiple_of(p * W_GROUP, W_GROUP)
        _gate_tiles(gt_ref, it_ref, jt_ref, tile_ref, base, base)
        return carry

    lax.fori_loop(0, tm // W_GROUP, token_group, 0)

    def key_rows(i2, carry):
        for d in range(W_ROWS):
            _tile_rows_out(tile_ref, w_ref, tm, i2 * W_ROWS + d)
        return carry

    lax.fori_loop(0, N_KEYS // W_ROWS, key_rows, 0)


def peer_w(gate, eid, tm=128):
    ns, T = gate.shape
    tm = math.gcd(T, tm)
    return pl.pallas_call(
        _peer_w_kernel,
        grid=(T // tm,),
        in_specs=[pl.BlockSpec((ns, tm), lambda i: (0, i)), pl.BlockSpec((ns, tm), lambda i: (0, i))],
        out_specs=pl.BlockSpec((tm, N_EXPERTS), lambda i: (i, 0)),
        out_shape=jax.ShapeDtypeStruct((T, N_EXPERTS), BF16),
        scratch_shapes=[pltpu.VMEM((tm, ns), F32), pltpu.VMEM((tm, ns), jnp.int32),
                        pltpu.VMEM((tm, ns), jnp.int32), pltpu.VMEM((tm * W_PITCH, N_KEYS), F32)],
        compiler_params=_params("parallel"),
        name="peer_gate_matrix",
    )(gate, eid)


def _dense_init(o_ref):
    @pl.when(pl.program_id(1) == 0)
    def _():
        o_ref[...] = jnp.zeros_like(o_ref)


def _dense_step(xb_ref, ut_ref, v_ref, w_ref, o_ref):
    h = _dot(xb_ref[...], ut_ref[...])
    act = 0.5 * h * (1.0 + lax.erf(h * (1.0 / math.sqrt(2.0))))
    hw = act * w_ref[...].astype(F32)
    o_ref[...] += _dot(hw.astype(BF16), v_ref[...])
    return hw


def _after(x, marker):
    z = pltpu.bitcast(marker, jnp.uint32)
    z = lax.shift_right_logical(lax.shift_right_logical(z, jnp.uint32(31)), jnp.uint32(1))
    z = pltpu.bitcast(z, F32)
    return x + jnp.tile(z, (x.shape[0] // z.shape[0], x.shape[1] // z.shape[1]))


def _peer_dense_kernel(xb_ref, ut_ref, v_ref, w_ref, o_ref):
    _dense_init(o_ref)
    _dense_step(xb_ref, ut_ref, v_ref, w_ref, o_ref)


def _peer_fused_kernel(*refs, with_gate, with_topk):
    refs = list(refs)
    xb_ref, ut_ref, v_ref, w_ref = refs[:4]
    del refs[:4]
    if with_gate:
        sg_ref, se_ref = refs[:2]
        del refs[:2]
    if with_topk:
        xn_ref, wq_ref, keys_ref = refs[:3]
        del refs[:3]
    o_ref = refs.pop(0)
    if with_gate:
        wn_ref = refs.pop(0)
    if with_topk:
        gate_ref, eid_ref = refs[:2]
        del refs[:2]
    if with_gate:
        gt_ref, it_ref, jt_ref, tile_ref = refs
    j = pl.program_id(1)

    @pl.when(j == 0)
    def _():
        o_ref[...] = jnp.zeros_like(o_ref)
        if with_gate:
            _selection_rows(sg_ref, se_ref, gt_ref, it_ref, jt_ref)

    if with_topk:
        half = PEER_DK // 2
        q = _dot(xn_ref[...], wq_ref[...]).astype(BF16)
        scores = [_dot_nt(keys_ref[c], q[:, c * half:(c + 1) * half]) for c in range(2)]
    hw = _dense_step(xb_ref, ut_ref, v_ref, w_ref, o_ref)
    if with_gate:
        ntok = wn_ref.shape[0]
        sel0 = pl.multiple_of(j * ntok, ntok)
        for g in range(0, ntok, W_GROUP):
            _gate_tiles(gt_ref, it_ref, jt_ref, tile_ref, sel0 + g, g)
        for i in range(N_KEYS):
            _tile_rows_out(tile_ref, wn_ref, ntok, i)
    if with_topk:
        marker = hw[hw.shape[0] - 8:, :LANES]
        row = pl.multiple_of(j * PEER_TOPK, PEER_TOPK)
        tm = xn_ref.shape[0]
        sub = math.gcd(tm, FUSE_LANES)
        for t0 in range(0, tm, sub):
            top = [_top_rows(_after(scores[c][:, t0:t0 + sub], marker), PEER_TOPK) for c in range(2)]
            gate, eid = _combine_halves([top[0][0], top[1][0]], [top[0][1], top[1][1]])
            gate_ref[pl.ds(row, PEER_TOPK), t0:t0 + sub] = gate
            eid_ref[pl.ds(row, PEER_TOPK), t0:t0 + sub] = eid
            marker = gate[:8]


def _dense_specs(tm, ne):
    return [pl.BlockSpec((tm, D_MODEL), lambda i, j: (i, 0)),
            pl.BlockSpec((D_MODEL, ne), lambda i, j: (0, j)),
            pl.BlockSpec((ne, D_MODEL), lambda i, j: (j, 0)),
            pl.BlockSpec((tm, ne), lambda i, j: (i, j))]


def peer_dense(xb, ut, v, wmat, tm=512, ne=N_EXPERTS // PEER_HEADS):
    T = xb.shape[0]
    tm = math.gcd(T, tm)
    return pl.pallas_call(
        _peer_dense_kernel,
        grid=(T // tm, N_EXPERTS // ne),
        in_specs=_dense_specs(tm, ne),
        out_specs=pl.BlockSpec((tm, D_MODEL), lambda i, j: (i, 0)),
        out_shape=jax.ShapeDtypeStruct((T, D_MODEL), F32),
        compiler_params=_params("parallel", "arbitrary"),
        name="peer_dense",
    )(xb, ut, v, wmat)


def peer_fused(xb, ut, v, wmat, sel_next=None, xn=None, wq=None, keys=None, tm=512):
    T = xb.shape[0]
    tm = math.gcd(T, tm)
    ne = N_EXPERTS // PEER_HEADS
    ns = PEER_HEADS * PEER_TOPK
    with_gate, with_topk = sel_next is not None, xn is not None
    ntok = tm // PEER_HEADS
    tile = lambda i, j: (i, 0)
    sel = lambda i, j: (0, i)
    args = [xb, ut, v, wmat]
    in_specs = _dense_specs(tm, ne)
    out_specs = [pl.BlockSpec((tm, D_MODEL), tile)]
    out_shape = [jax.ShapeDtypeStruct((T, D_MODEL), F32)]
    scratch = []
    if with_gate:
        assert sel_next[0].shape == (ns, T) and ntok % W_GROUP == 0
        args += list(sel_next)
        in_specs += [pl.BlockSpec((ns, tm), sel), pl.BlockSpec((ns, tm), sel)]
        out_specs.append(pl.BlockSpec((ntok, N_EXPERTS), lambda i, j: (i * PEER_HEADS + j, 0)))
        out_shape.append(jax.ShapeDtypeStruct((T, N_EXPERTS), BF16))
        scratch = [pltpu.VMEM((tm, ns), F32), pltpu.VMEM((tm, ns), jnp.int32), pltpu.VMEM((tm, ns), jnp.int32),
                   pltpu.VMEM((ntok * W_PITCH, N_KEYS), F32)]
    if with_topk:
        assert xn.shape[0] == T
        args += [xn, wq, keys]
        in_specs += [pl.BlockSpec((tm, D_MODEL), tile), pl.BlockSpec((D_MODEL, PEER_DK), lambda i, j: (0, j)),
                     _full(keys.shape)]
        out_specs += [pl.BlockSpec((ns, tm), sel), pl.BlockSpec((ns, tm), sel)]
        out_shape += [jax.ShapeDtypeStruct((ns, T), F32), jax.ShapeDtypeStruct((ns, T), jnp.int32)]
    return pl.pallas_call(
        functools.partial(_peer_fused_kernel, with_gate=with_gate, with_topk=with_topk),
        grid=(T // tm, PEER_HEADS),
        in_specs=in_specs,
        out_specs=out_specs,
        out_shape=out_shape,
        scratch_shapes=scratch,
        compiler_params=_params("parallel", "arbitrary"),
        name="peer_fused",
    )(*args)


def _ln_ple_kernel(x_ref, f_ref, p_ref, g_ref, b_ref, wg_ref, bg_ref, wp_ref, o_ref, *, alpha):
    y = _layer_norm(alpha * x_ref[...] + f_ref[...], g_ref[...], b_ref[...])
    gate = 1.0 / (1.0 + jnp.exp(-(_dot(y.astype(BF16), wg_ref[...]) + bg_ref[...])))
    o_ref[...] = y + gate * _dot(p_ref[...].astype(BF16), wp_ref[...])


def ln_ple(x2d, f2d, p2d, p_row0, g, b, wg, bg, wp, alpha, tm=512):
    T = x2d.shape[0]
    tm = math.gcd(math.gcd(T, tm), p_row0) if p_row0 else math.gcd(T, tm)
    blk0 = p_row0 // tm
    row = lambda i: (i, 0)
    return pl.pallas_call(
        functools.partial(_ln_ple_kernel, alpha=alpha),
        grid=(T // tm,),
        in_specs=[pl.BlockSpec((tm, D_MODEL), row), pl.BlockSpec((tm, D_MODEL), row),
                  pl.BlockSpec((tm, PLE_DIM), lambda i: (i + blk0, 0)),
                  _full(g.shape), _full(b.shape), _full(wg.shape), _full(bg.shape), _full(wp.shape)],
        out_specs=pl.BlockSpec((tm, D_MODEL), row),
        out_shape=jax.ShapeDtypeStruct((T, D_MODEL), F32),
        compiler_params=_params("parallel"),
        name="ln_ple",
    )(x2d, f2d, p2d, g, b, wg, bg, wp)


def _rot_cols(w):
    half = MLA_ROPE // 2
    return jnp.concatenate([-w[..., half:], w[..., :half]], axis=-1)


def _mla_weights(w_a, q_norm, kv_norm, w_uq, w_ukv):
    zk = jnp.zeros((D_MODEL, LANES - MLA_ROPE), F32)
    w_kr = w_a[:, MLA_Q_LORA + MLA_KV_LORA:]
    wa = jnp.concatenate([w_a[:, :MLA_Q_LORA + MLA_KV_LORA], w_kr, zk, _rot_cols(w_kr), zk], axis=1)
    uq = w_uq.reshape(MLA_Q_LORA, MLA_HEADS, MLA_NOPE + MLA_ROPE)
    zq = jnp.zeros((MLA_Q_LORA, MLA_HEADS, MLA_QK_PAD - MLA_NOPE - MLA_ROPE), F32)
    wuq = jnp.concatenate([uq, zq], axis=-1).reshape(MLA_Q_LORA, MLA_HEADS * MLA_QK_PAD)
    wuqr = jnp.concatenate([jnp.zeros_like(uq[..., :MLA_NOPE]), _rot_cols(uq[..., MLA_NOPE:]), zq],
                           axis=-1).reshape(MLA_Q_LORA, MLA_HEADS * MLA_QK_PAD)
    ukv = w_ukv.reshape(MLA_KV_LORA, MLA_HEADS, MLA_NOPE + MLA_V)
    return {
        "wa": wa.astype(BF16),
        "qn": q_norm.astype(F32).reshape(1, -1),
        "kvn": kv_norm.astype(F32).reshape(1, -1),
        "wuq": wuq.astype(BF16),
        "wuqr": wuqr.astype(BF16),
        "wuk": ukv[..., :MLA_NOPE].reshape(MLA_KV_LORA, -1).astype(BF16),
        "wuv": ukv[..., MLA_NOPE:].reshape(MLA_KV_LORA, -1).astype(BF16),
    }


def _rope_tabs(seq):
    inv_freq = 1.0 / (ROPE_THETA ** (jnp.arange(0, MLA_ROPE, 2, dtype=F32) / MLA_ROPE))
    ang = jnp.arange(seq, dtype=F32)[:, None] * inv_freq[None, :]
    emb = jnp.concatenate([ang, ang], axis=-1)
    cos, sin = jnp.cos(emb), jnp.sin(emb)
    one = jnp.ones((seq, MLA_NOPE), F32)
    z64 = jnp.zeros((seq, LANES - MLA_ROPE), F32)
    z128 = jnp.zeros((seq, MLA_NOPE), F32)
    return (jnp.concatenate([one, cos, z64], axis=1), jnp.concatenate([z128, sin, z64], axis=1),
            jnp.concatenate([cos, z64], axis=1), jnp.concatenate([sin, z64], axis=1))


def _mixer(i, x2d, batch, seq, lbs, mla_w, hgrn_w, ln_w, alpha, tabs):
    j = i // 2
    g1, b1, _, _ = ln_w[i]
    if i % 2 == 0:
        w = mla_w[j]
        q2d, k2d, v2d = mla_proj(x2d, seq, w, tabs)
        o2d = attention(q2d, k2d.T, v2d, batch, seq)
        return proj_res_ln(o2d, x2d, w["wo"], g1, b1, alpha)
    w = hgrn_w[j]
    o2 = hgrn_scan(x2d, w["wqzv"], lbs[:, j], batch, seq)
    return hgrn_out(o2, w["wgate"], x2d, w["nw"], w["wo"], g1, b1, alpha)


def _trunk(groups, lbs, mla_w, hgrn_w, peer_w_, ln_w, ple_w, depth):
    alpha = (2 * depth) ** 0.25
    n = len(groups)
    total = depth * n
    tabs = {s: _rope_tabs(s) for s in {g["seq"] for g in groups}}
    xin = {k: groups[k]["x"] for k in range(n)}
    memo = {}

    def mixed(m):
        if m not in memo:
            g = groups[m % n]
            memo[m] = _mixer(m // n, xin[m], g["batch"], g["seq"], lbs, mla_w, hgrn_w, ln_w, alpha, tabs[g["seq"]])
        return memo[m]

    def finish(m, f):
        i, k = divmod(m, n)
        g = groups[k]
        _, _, g2, b2 = ln_w[i]
        wg, bg, wp = ple_w[i]
        return ln_ple(mixed(m)[0], f, g["p"][i], g["row0"], g2, b2, wg, bg, wp, alpha)

    pw = lambda m: peer_w_[m // n]
    pipelined = n > 2 and all(g["x"].shape[0] == groups[0]["x"].shape[0] for g in groups)
    sel, wmat = {}, {}
    if pipelined:
        for m in range(2):
            sel[m] = peer_topk(mixed(m)[1], pw(m)["wq"], pw(m)["keys"])
        wmat[0] = peer_w(*sel[0])
    for m in range(total):
        x1b = mixed(m)[1]
        if pipelined:
            sel_next = sel.pop(m + 1) if m + 1 < total else None
            xn = mixed(m + 2)[1] if m + 2 < total else None
            pn = pw(min(m + 2, total - 1))
            res = list(peer_fused(x1b, pw(m)["ut"], pw(m)["v"], wmat.pop(m), sel_next, xn, pn["wq"], pn["keys"]))
            f = res.pop(0)
            if sel_next is not None:
                wmat[m + 1] = res.pop(0)
            if xn is not None:
                sel[m + 2] = (res[0], res[1])
        else:
            f = peer_dense(x1b, pw(m)["ut"], pw(m)["v"], peer_w(*peer_topk(x1b, pw(m)["wq"], pw(m)["keys"])))
        xin[m + n] = finish(m, f)
        del memo[m], xin[m]
    return [xin[total + k] for k in range(n)]


def _token_groups(x, p, target):
    batch, seq, _ = x.shape
    depth = p.shape[0]
    T = batch * seq
    x2d = x.reshape(T, D_MODEL)
    p2d = p.reshape(depth, T, PLE_DIM)
    n = T // target if (target % seq == 0 and T % target == 0) else 1
    tg = T // n
    return [{"x": x2d if n == 1 else x2d[k * tg:(k + 1) * tg], "p": p2d, "row0": k * tg,
             "batch": batch // n, "seq": seq} for k in range(n)]


def kernel(x_prompt, x_sample, p_prompt, p_sample, mla_w_a, mla_q_norm, mla_kv_norm, mla_w_uq, mla_w_ukv, mla_w_o, hgrn_w_in, hgrn_lb, hgrn_norm, hgrn_w_o, peer_w_q, peer_sub_keys, peer_u, peer_v, ln1_g, ln1_b, ln2_g, ln2_b, ple_gate_w, ple_gate_b, ple_proj):
    depth = peer_w_q.shape[0]
    lbs = lower_bounds(hgrn_lb)
    mla_w = []
    for j in range(mla_w_a.shape[0]):
        w = _mla_weights(mla_w_a[j], mla_q_norm[j], mla_kv_norm[j], mla_w_uq[j], mla_w_ukv[j])
        w["wo"] = mla_w_o[j].astype(BF16)
        mla_w.append(w)
    hk = HGRN_HEADS * HGRN_DK
    hgrn_w = []
    for j in range(hgrn_w_in.shape[0]):
        win = hgrn_w_in[j].astype(BF16)
        qcol, vcol = win[:, :hk], win[:, 3 * hk:3 * hk + HGRN_HEADS * HGRN_DV]
        hgrn_w.append({
            "wqzv": jnp.stack([jnp.concatenate([qcol, win[:, (1 + d) * hk:(2 + d) * hk], vcol], axis=1)
                               for d in range(2)]),
            "wgate": win[:, 3 * hk + HGRN_HEADS * HGRN_DV:],
            "nw": hgrn_norm[j].astype(F32).reshape(1, -1),
            "wo": hgrn_w_o[j].astype(BF16)})
    peer_w_ = [{"wq": peer_w_q[i].astype(BF16), "keys": peer_sub_keys[i].astype(BF16),
                "ut": peer_u[i].astype(BF16).T, "v": peer_v[i].astype(BF16)} for i in range(depth)]
    r = lambda a: a.astype(F32).reshape(1, -1)
    ln_w = [(r(ln1_g[i]), r(ln1_b[i]), r(ln2_g[i]), r(ln2_b[i])) for i in range(depth)]
    ple_w = [(ple_gate_w[i].astype(BF16), r(ple_gate_b[i]), ple_proj[i].astype(BF16)) for i in range(depth)]
    target = max(x_prompt.shape[1], x_sample.shape[1])
    g_prompt = _token_groups(x_prompt, p_prompt, target)
    groups = g_prompt + _token_groups(x_sample, p_sample, target)
    ys = _trunk(groups, lbs, mla_w, hgrn_w, peer_w_, ln_w, ple_w, depth)
    join = lambda parts: parts[0] if len(parts) == 1 else jnp.concatenate(parts, axis=0)
    y_prompt = join(ys[:len(g_prompt)]).reshape(x_prompt.shape)
    y_sample = join(ys[len(g_prompt):]).reshape(x_sample.shape)
    return (y_prompt, y_sample)
```
